```python
import jax, jax.numpy as jnp
from jax import lax
import numpy as np

D_MODEL = 2048
BATCH = 2
SEQ = 8192
DEPTH = 1

GRID_W = 64
CTX_LEN = 256
N_MOD = 6
SSD_HEAD_DIM = 64
SSD_HEADS = 32
SSD_WIDTH = SSD_HEADS * SSD_HEAD_DIM
SSD_GROUPS = 4
SSD_HPG = SSD_HEADS // SSD_GROUPS
SSD_STATE = 128
SSD_BC = SSD_GROUPS * SSD_STATE
SSD_CHUNK = 128
CONV_W = 4
CONV_PAD_LEFT = 2
LRU_WIDTH = D_MODEL
LRU_HEADS = 16
LRU_HEAD_DIM = LRU_WIDTH // LRU_HEADS
LRU_C = 8.0
FFN_HIDDEN = ((8 * D_MODEL + 2) // 3 + 255) // 256 * 256
IN_SIZES = (SSD_WIDTH, SSD_WIDTH, SSD_BC, SSD_BC, 2 * SSD_HEADS, LRU_WIDTH, LRU_WIDTH)
IN_COLS = sum(IN_SIZES)
IN_SPLITS = tuple(int(s) for s in np.cumsum(IN_SIZES)[:-1])
EPS = 1e-6

kernel_name = "hybrid_ssd_rglru_prefix_dit_block"


def rmsnorm(t, g):
    tf = t.astype(jnp.float32)
    tf = tf * lax.rsqrt(jnp.mean(tf * tf, axis=-1, keepdims=True) + EPS)
    return (tf * g.astype(jnp.float32)).astype(t.dtype)


def modulate(h, shift, scale):
    return h * (1 + scale) + shift


def flip(t):
    return jnp.flip(t, axis=1)


def dwconv_centred(t, w, bias):
    L = t.shape[1]
    tp = jnp.pad(t, ((0, 0), (CONV_PAD_LEFT, CONV_W - 1 - CONV_PAD_LEFT), (0, 0)))
    out = bias
    for k in range(CONV_W):
        out = out + w[k] * tp[:, k:k + L]
    return out


def to_col_major(t, rows):
    b_, L, C = t.shape
    return t.reshape(b_, rows, GRID_W, C).transpose(0, 2, 1, 3).reshape(b_, L, C)


def to_row_major(t, rows):
    b_, L, C = t.shape
    return t.reshape(b_, GRID_W, rows, C).transpose(0, 2, 1, 3).reshape(b_, L, C)


def _ssd_chunks(x, dt, A, B):
    b_, L, G, E, P = x.shape
    nc = L // SSD_CHUNK
    x = x.reshape(b_, nc, SSD_CHUNK, G, E, P)
    dt = dt.reshape(b_, nc, SSD_CHUNK, G, E)
    B = B.reshape(b_, nc, SSD_CHUNK, G, SSD_STATE)
    xdt = x * dt[..., None]
    a_cum = jnp.cumsum(dt * A, axis=2)
    decay_to_end = jnp.exp(a_cum[:, :, -1:] - a_cum)
    states = jnp.einsum('bcqgn,bcqge,bcqgep->bcgepn', B, decay_to_end, xdt)
    chunk_decay = jnp.exp(a_cum[:, :, -1])
    return xdt, B, a_cum, states, chunk_decay


def _carry_states(states, chunk_decay, h0):
    def step(h, inp):
        s, d = inp
        return d[..., None, None] * h + s, h
    final, entering = lax.scan(step, h0, (jnp.moveaxis(states, 1, 0), jnp.moveaxis(chunk_decay, 1, 0)))
    return jnp.moveaxis(entering, 0, 1), final


def ssd_scan(x, dt, A, B, C, h0):
    b_, L, G, E, P = x.shape
    xdt, Bc, a_cum, states, chunk_decay = _ssd_chunks(x, dt, A, B)
    entering, final = _carry_states(states, chunk_decay, h0)
    Cc = C.reshape(Bc.shape)
    idx = jnp.arange(SSD_CHUNK)
    lower = (idx[:, None] >= idx[None, :])[None, None, :, :, None, None]
    seg = a_cum[:, :, :, None] - a_cum[:, :, None, :]
    Lmat = jnp.exp(jnp.where(lower, seg, -jnp.inf))
    CB = jnp.einsum('bcign,bcjgn->bcijg', Cc, Bc)
    y_diag = jnp.einsum('bcijg,bcijge,bcjgep->bcigep', CB, Lmat, xdt)
    y_off = jnp.einsum('bcign,bcige,bcgepn->bcigep', Cc, jnp.exp(a_cum), entering)
    return (y_diag + y_off).reshape(b_, L, G, E, P), final


def ssd_final_state(x, dt, A, B, h0):
    _, _, _, states, chunk_decay = _ssd_chunks(x, dt, A, B)
    _, final = _carry_states(states, chunk_decay, h0)
    return final


def ssd_inputs(xs, bs, cs, dt_raw, lp):
    f32 = jnp.float32
    xbc = jax.nn.silu(dwconv_centred(jnp.concatenate([xs, bs, cs], axis=-1), lp['ssd_conv_w'], lp['ssd_conv_b']))
    xs, bs, cs = jnp.split(xbc, [SSD_WIDTH, SSD_WIDTH + SSD_BC], axis=-1)
    b_, L = xs.shape[:2]
    xs = xs.reshape(b_, L, SSD_GROUPS, SSD_HPG, SSD_HEAD_DIM).astype(f32)
    bs = bs.reshape(b_, L, SSD_GROUPS, SSD_STATE).astype(f32)
    cs = cs.reshape(b_, L, SSD_GROUPS, SSD_STATE).astype(f32)
    dt = jax.nn.softplus(dt_raw.astype(f32).reshape(b_, L, 2, SSD_HEADS) + lp['ssd_dt_bias'].astype(f32))
    dt = dt.reshape(b_, L, 2, SSD_GROUPS, SSD_HPG)
    A = -jnp.exp(lp['ssd_a_log'].astype(f32)).reshape(2, SSD_GROUPS, SSD_HPG)
    return xs, bs, cs, dt, A


def lru_coeffs(xr, w_a, b_a, w_x, b_x, lam):
    b_, L, W = xr.shape
    xh = xr.reshape(b_, L, LRU_HEADS, LRU_HEAD_DIM)
    f32 = jnp.float32
    r = jax.nn.sigmoid(jnp.einsum('blhi,hij->blhj', xh, w_a.astype(f32)).reshape(b_, L, W) + b_a.astype(f32))
    i = jax.nn.sigmoid(jnp.einsum('blhi,hij->blhj', xh, w_x.astype(f32)).reshape(b_, L, W) + b_x.astype(f32))
    log_a = -LRU_C * r * jax.nn.softplus(-lam.astype(f32))
    a = jnp.exp(log_a)
    return a, jnp.sqrt(-jnp.expm1(2 * log_a)) * (i * xr)


def linear_scan(a, u, h0):
    u = u.at[:, 0].add(a[:, 0] * h0)
    def comb(lhs, rhs):
        return lhs[0] * rhs[0], rhs[0] * lhs[1] + rhs[1]
    _, h = lax.associative_scan(comb, (a, u), axis=1)
    return h


def lru_bidir(xr_seq, lp, h0_f, h0_b):
    a_f, u_f = lru_coeffs(xr_seq, lp['lru_w_a'][0], lp['lru_b_a'][0], lp['lru_w_x'][0], lp['lru_b_x'][0], lp['lru_lambda'][0])
    a_b, u_b = lru_coeffs(xr_seq, lp['lru_w_a'][1], lp['lru_b_a'][1], lp['lru_w_x'][1], lp['lru_b_x'][1], lp['lru_lambda'][1])
    h_f = linear_scan(a_f, u_f, h0_f)
    h_b_rev = linear_scan(flip(a_b), flip(u_b), h0_b)
    return h_f + flip(h_b_rev), h_f[:, -1], h_b_rev[:, -1]


def mixer_full(h, lp, init, rows):
    z, xs, bs, cs, dt_raw, xr, yr = jnp.split(h @ lp['w_in'], IN_SPLITS, axis=-1)
    b_, L = h.shape[:2]
    xsh, bsh, csh, dt, A = ssd_inputs(xs, bs, cs, dt_raw, lp)
    y_f, s_f = ssd_scan(xsh, dt[:, :, 0], A[0], bsh, csh, init[0])
    y_b, s_b = ssd_scan(flip(xsh), flip(dt[:, :, 1]), A[1], flip(bsh), flip(csh), init[1])
    d_skip = lp['ssd_d'].astype(jnp.float32).reshape(SSD_GROUPS, SSD_HPG)[..., None]
    y = (y_f + flip(y_b) + d_skip * xsh).reshape(b_, L, SSD_WIDTH)
    y = rmsnorm(y * jax.nn.silu(z.astype(jnp.float32)), lp['ssd_norm'])
    o_s = y.astype(h.dtype) @ lp['w_out_ssd']
    xr_seq = xr if rows is None else to_col_major(xr, rows)
    xr_seq = dwconv_centred(xr_seq, lp['lru_conv_w'], lp['lru_conv_b']).astype(jnp.float32)
    r_out, f_f, f_b = lru_bidir(xr_seq, lp, init[2], init[3])
    if rows is not None:
        r_out = to_row_major(r_out, rows)
    o_r = (r_out * jax.nn.gelu(yr.astype(jnp.float32))).astype(h.dtype) @ lp['w_out_lru']
    g_s, g_r = jnp.split(jax.nn.sigmoid(h @ lp['w_gate'] + lp['b_gate']), 2, axis=-1)
    out = (g_s * o_s + g_r * o_r) @ lp['w_o']
    return out, (s_f, s_b, f_f, f_b)


def mixer_states(hc, lp):
    _, xs, bs, cs, dt_raw, xr, _ = jnp.split(hc @ lp['w_in'], IN_SPLITS, axis=-1)
    b_ = hc.shape[0]
    xsh, bsh, _, dt, A = ssd_inputs(xs, bs, cs, dt_raw, lp)
    s0 = jnp.zeros((b_, SSD_GROUPS, SSD_HPG, SSD_HEAD_DIM, SSD_STATE), jnp.float32)
    s_f = ssd_final_state(xsh, dt[:, :, 0], A[0], bsh, s0)
    s_b = ssd_final_state(flip(xsh), flip(dt[:, :, 1]), A[1], flip(bsh), s0)
    xr_seq = dwconv_centred(xr, lp['lru_conv_w'], lp['lru_conv_b']).astype(jnp.float32)
    h0 = jnp.zeros((b_, LRU_WIDTH), jnp.float32)
    _, f_f, f_b = lru_bidir(xr_seq, lp, h0, h0)
    return (s_f, s_b, f_f, f_b)


def zero_states(b_):
    s0 = jnp.zeros((b_, SSD_GROUPS, SSD_HPG, SSD_HEAD_DIM, SSD_STATE), jnp.float32)
    h0 = jnp.zeros((b_, LRU_WIDTH), jnp.float32)
    return (s0, s0, h0, h0)


def swiglu(h, w13, w2):
    g, u = jnp.split(h @ w13, 2, axis=-1)
    return (jax.nn.silu(g) * u) @ w2


def setup_inputs(seed: int = 0) -> dict:
    key = jax.random.key(seed)
    ks = jax.random.split(key, 30)
    D = D_MODEL
    f32 = jnp.float32

    def nrm(k, shape, fan_in, s=1.0):
        return jax.random.normal(k, shape, f32) * (s * fan_in ** -0.5)

    def small(k, shape, s=0.02):
        return jax.random.normal(k, shape, f32) * s

    dt0 = jnp.exp(jax.random.uniform(ks[10], (DEPTH, 2, SSD_HEADS), f32, np.log(1e-3), np.log(1e-1)))
    a_target = jax.random.uniform(ks[21], (DEPTH, 2, LRU_WIDTH), f32, 0.9, 0.999)
    sig_lam = a_target ** (1.0 / LRU_C)
    return {
        "x": jax.random.normal(ks[0], (BATCH, SEQ, D), f32),
        "c": jax.random.normal(ks[1], (BATCH, D), f32),
        "ctx": jax.random.normal(ks[2], (BATCH, CTX_LEN, D), f32),
        "c_ctx": jax.random.normal(ks[3], (D,), f32),
        "w_ada": nrm(ks[4], (DEPTH, D, N_MOD * D), D, 0.5),
        "b_ada": small(ks[5], (DEPTH, N_MOD * D)),
        "norm_mix": 1.0 + small(ks[6], (DEPTH, D)),
        "norm_ffn": 1.0 + small(ks[7], (DEPTH, D)),
        "w_in": nrm(ks[8], (DEPTH, D, IN_COLS), D),
        "ssd_conv_w": nrm(ks[9], (DEPTH, CONV_W, SSD_WIDTH + 2 * SSD_BC), CONV_W),
        "ssd_conv_b": small(ks[11], (DEPTH, SSD_WIDTH + 2 * SSD_BC)),
        "ssd_dt_bias": dt0 + jnp.log(-jnp.expm1(-dt0)),
        "ssd_a_log": jnp.log(jax.random.uniform(ks[12], (DEPTH, 2, SSD_HEADS), f32, 1.0, 16.0)),
        "ssd_d": 1.0 + small(ks[13], (DEPTH, SSD_HEADS)),
        "ssd_norm": 1.0 + small(ks[14], (DEPTH, SSD_WIDTH)),
        "w_out_ssd": nrm(ks[15], (DEPTH, SSD_WIDTH, D), SSD_WIDTH),
        "lru_conv_w": nrm(ks[16], (DEPTH, CONV_W, LRU_WIDTH), CONV_W),
        "lru_conv_b": small(ks[17], (DEPTH, LRU_WIDTH)),
        "lru_w_a": nrm(ks[18], (DEPTH, 2, LRU_HEADS, LRU_HEAD_DIM, LRU_HEAD_DIM), LRU_HEAD_DIM),
        "lru_b_a": small(ks[19], (DEPTH, 2, LRU_WIDTH)),
        "lru_w_x": nrm(ks[20], (DEPTH, 2, LRU_HEADS, LRU_HEAD_DIM, LRU_HEAD_DIM), LRU_HEAD_DIM),
        "lru_b_x": small(ks[22], (DEPTH, 2, LRU_WIDTH)),
        "lru_lambda": jnp.log(sig_lam) - jnp.log1p(-sig_lam),
        "w_out_lru": nrm(ks[23], (DEPTH, LRU_WIDTH, D), LRU_WIDTH),
        "w_gate": nrm(ks[24], (DEPTH, D, 2 * D), D),
        "b_gate": small(ks[25], (DEPTH, 2 * D)),
        "w_o": nrm(ks[26], (DEPTH, D, D), D),
        "ffn_w13": nrm(ks[27], (DEPTH, D, 2 * FFN_HIDDEN), D),
        "ffn_w2": nrm(ks[28], (DEPTH, FFN_HIDDEN, D), FFN_HIDDEN),
        "final_norm": 1.0 + small(ks[29], (D,)),
    }


def reference(x, c, ctx, c_ctx, w_ada, b_ada, norm_mix, norm_ffn, w_in, ssd_conv_w, ssd_conv_b,
              ssd_dt_bias, ssd_a_log, ssd_d, ssd_norm, w_out_ssd, lru_conv_w, lru_conv_b,
              lru_w_a, lru_b_a, lru_w_x, lru_b_x, lru_lambda, w_out_lru, w_gate, b_gate, w_o,
              ffn_w13, ffn_w2, final_norm):
    rows = x.shape[1] // GRID_W
    for l in range(DEPTH):
        lp = dict(w_in=w_in[l], ssd_conv_w=ssd_conv_w[l], ssd_conv_b=ssd_conv_b[l],
                  ssd_dt_bias=ssd_dt_bias[l], ssd_a_log=ssd_a_log[l], ssd_d=ssd_d[l],
                  ssd_norm=ssd_norm[l], w_out_ssd=w_out_ssd[l], lru_conv_w=lru_conv_w[l],
                  lru_conv_b=lru_conv_b[l], lru_w_a=lru_w_a[l], lru_b_a=lru_b_a[l],
                  lru_w_x=lru_w_x[l], lru_b_x=lru_b_x[l], lru_lambda=lru_lambda[l],
                  w_out_lru=w_out_lru[l], w_gate=w_gate[l], b_gate=b_gate[l], w_o=w_o[l])
        mod = jax.nn.silu(c) @ w_ada[l] + b_ada[l]
        sh_m, sc_m, g_m, sh_f, sc_f, g_f = jnp.split(mod[:, None, :], N_MOD, axis=-1)
        mod_c = jax.nn.silu(c_ctx) @ w_ada[l] + b_ada[l]
        csh_m, csc_m, cg_m, csh_f, csc_f, cg_f = jnp.split(mod_c, N_MOD, axis=-1)
        hc = modulate(rmsnorm(ctx, norm_mix[l]), csh_m, csc_m)
        if l + 1 < DEPTH:
            mix_c, ctx_states = mixer_full(hc, lp, zero_states(ctx.shape[0]), None)
            ctx = ctx + cg_m * mix_c
            ctx = ctx + cg_f * swiglu(modulate(rmsnorm(ctx, norm_ffn[l]), csh_f, csc_f), ffn_w13[l], ffn_w2[l])
        else:
            ctx_states = mixer_states(hc, lp)
        h = modulate(rmsnorm(x, norm_mix[l]), sh_m, sc_m)
        mix, _ = mixer_full(h, lp, ctx_states, rows)
        x = x + g_m * mix
        h = modulate(rmsnorm(x, norm_ffn[l]), sh_f, sc_f)
        x = x + g_f * swiglu(h, ffn_w13[l], ffn_w2[l])
    return rmsnorm(x, final_norm)
```

```python
import functools

import jax
import jax.numpy as jnp
from jax import lax
from jax.experimental import pallas as pl
from jax.experimental.pallas import tpu as pltpu

F32 = jnp.float32
BF16 = jnp.bfloat16

D_MODEL = 2048
GRID_W = 64
N_MOD = 6
SSD_HEAD_DIM = 64
SSD_HEADS = 32
SSD_WIDTH = SSD_HEADS * SSD_HEAD_DIM
SSD_GROUPS = 4
SSD_HPG = SSD_HEADS // SSD_GROUPS
SSD_STATE = 128
SSD_BC = SSD_GROUPS * SSD_STATE
SSD_CHUNK = 128
CONV_W = 4
CONV_PAD_LEFT = 2
LRU_WIDTH = D_MODEL
LRU_HEADS = 16
LRU_HEAD_DIM = LRU_WIDTH // LRU_HEADS
LRU_C = 8.0
EPS = 1e-6

LANES = 128
HALO = 16
VMEM_LIMIT = 56 * 1024 * 1024

COL_Z = 0
COL_XS = 2048
COL_XR = 4096
COL_YR = 6144
COL_GS = 8192
COL_GR = 10240
COL_BC = 12288
NP = 13312
DT_COLS = 256


def _params(sem, vmem=VMEM_LIMIT):
    return pltpu.CompilerParams(dimension_semantics=sem, vmem_limit_bytes=vmem)


def _sigmoid(x):
    return 1.0 / (1.0 + jnp.exp(-x))


def _softplus(x):
    return jnp.maximum(x, 0.0) + jnp.log1p(jnp.exp(-jnp.abs(x)))


def _norm_mod(x, g, shift, scale):
    xf = x.astype(F32)
    ms = jnp.mean(xf * xf, axis=-1, keepdims=True)
    hn = xf * lax.rsqrt(ms + EPS) * g
    return hn * (1.0 + scale) + shift


def _ada_kernel(c_ref, w_ref, b_ref, o_ref):
    c = c_ref[...]
    s = (c * _sigmoid(c)).astype(BF16)
    o_ref[...] = jnp.dot(s, w_ref[...].astype(BF16), preferred_element_type=F32) + b_ref[...]


def _ada(cvec, w, b):
    m, d = cvec.shape
    n = w.shape[1]
    tn = 1024
    return pl.pallas_call(
        _ada_kernel,
        out_shape=jax.ShapeDtypeStruct((m, n), F32),
        grid=(n // tn,),
        in_specs=[pl.BlockSpec((m, d), lambda j: (0, 0)),
                  pl.BlockSpec((d, tn), lambda j: (0, j)),
                  pl.BlockSpec((1, tn), lambda j: (0, j))],
        out_specs=pl.BlockSpec((m, tn), lambda j: (0, j)),
        compiler_params=_params(("arbitrary",)),
        name="ada",
    )(cvec, w, b)


def _in_proj_kernel(x_ref, sh_ref, sc_ref, g_ref, w_ref, b_ref, wdt_ref, p_ref, dt_ref, hn_ref,
                    *, gate_lo, gate_hi):
    n = pl.program_id(2)

    @pl.when(n == 0)
    def _():
        h = _norm_mod(x_ref[...], g_ref[...], sh_ref[...], sc_ref[...]).astype(BF16)
        hn_ref[...] = h
        dt_ref[...] = jnp.dot(h, wdt_ref[...], preferred_element_type=F32)

    acc = jnp.dot(hn_ref[...], w_ref[...], preferred_element_type=F32) + b_ref[...]
    is_gate = jnp.logical_and(n >= gate_lo, n < gate_hi)

    @pl.when(is_gate)
    def _():
        p_ref[...] = _sigmoid(acc).astype(BF16)

    @pl.when(jnp.logical_not(is_gate))
    def _():
        p_ref[...] = acc.astype(BF16)


def _in_proj(x, shift, scale, g, w_cat, b_cat, w_dt, tm):
    bsz, s, d = x.shape
    tn = 1024
    kern = functools.partial(_in_proj_kernel, gate_lo=COL_GS // tn, gate_hi=COL_BC // tn)
    return pl.pallas_call(
        kern,
        out_shape=(jax.ShapeDtypeStruct((bsz, s, NP), BF16),
                   jax.ShapeDtypeStruct((bsz, s, DT_COLS), F32)),
        grid=(bsz, s // tm, NP // tn),
        in_specs=[pl.BlockSpec((None, tm, d), lambda b, i, n: (b, i, 0)),
                  pl.BlockSpec((None, 1, d), lambda b, i, n: (b, 0, 0)),
                  pl.BlockSpec((None, 1, d), lambda b, i, n: (b, 0, 0)),
                  pl.BlockSpec((1, d), lambda b, i, n: (0, 0)),
                  pl.BlockSpec((d, tn), lambda b, i, n: (0, n)),
                  pl.BlockSpec((1, tn), lambda b, i, n: (0, n)),
                  pl.BlockSpec((d, DT_COLS), lambda b, i, n: (0, 0))],
        out_specs=(pl.BlockSpec((None, tm, tn), lambda b, i, n: (b, i, n)),
                   pl.BlockSpec((None, tm, DT_COLS), lambda b, i, n: (b, i, 0))),
        scratch_shapes=[pltpu.VMEM((tm, d), BF16)],
        compiler_params=_params(("arbitrary", "arbitrary", "arbitrary")),
        name="in_proj",
    )(x, shift, scale, g, w_cat, b_cat, w_dt)


def _split3(v):
    v1 = v.astype(BF16)
    r1 = v - v1.astype(F32)
    v2 = r1.astype(BF16)
    r2 = r1 - v2.astype(F32)
    return v1, v2, r2.astype(BF16)


def _ssd_kernel(*refs, rev, final, nc):
    if final:
        (xs_ref, xsp_ref, xsn_ref, bc_ref, bcp_ref, bcn_ref, dt_ref,
         cwx_ref, cbx_ref, cwbc_ref, cbbc_ref, dtb_ref, alog_ref, s0_ref,
         yf_ref, z_ref, dskip_ref, nw_ref,
         y_ref, sfin_ref, st_ref, extx_ref, extbc_ref, gat_ref) = refs
    else:
        (xs_ref, xsp_ref, xsn_ref, bc_ref, bcp_ref, bcn_ref, dt_ref,
         cwx_ref, cbx_ref, cwbc_ref, cbbc_ref, dtb_ref, alog_ref, s0_ref,
         y_ref, sfin_ref, st_ref, extx_ref, extbc_ref) = refs

    i = pl.program_id(1)
    c = (nc - 1 - i) if rev else i
    q = SSD_CHUNK

    @pl.when(i == 0)
    def _():
        st_ref[...] = s0_ref[...]

    has_prev = (c > 0).astype(F32)
    has_next = (c < nc - 1).astype(F32)

    def conv_silu(cur_ref, prev_ref, next_ref, ext_ref, w_ref, b_ref):
        ext_ref[0:HALO, :] = prev_ref[...].astype(F32) * has_prev
        ext_ref[HALO:HALO + q, :] = cur_ref[...].astype(F32)
        ext_ref[HALO + q:HALO + q + HALO, :] = next_ref[...].astype(F32) * has_next
        out = b_ref[...]
        for k in range(CONV_W):
            lo = HALO - CONV_PAD_LEFT + k
            out = out + w_ref[k:k + 1, :] * ext_ref[lo:lo + q, :]
        return out * _sigmoid(out)

    xs = conv_silu(xs_ref, xsp_ref, xsn_ref, extx_ref, cwx_ref, cbx_ref)
    bcv = conv_silu(bc_ref, bcp_ref, bcn_ref, extbc_ref, cwbc_ref, cbbc_ref)

    dt = _softplus(dt_ref[...] + dtb_ref[...])
    a = dt * (-jnp.exp(alog_ref[...]))

    ii = lax.broadcasted_iota(jnp.int32, (q, q), 0)
    jj = lax.broadcasted_iota(jnp.int32, (q, q), 1)
    mask = (jj >= ii) if rev else (jj <= ii)
    tri = jnp.where(mask, 1.0, 0.0).astype(BF16)

    acum = sum(jnp.dot(tri, p, preferred_element_type=F32) for p in _split3(a))
    a_t = a.T
    dt_t = dt.T
    nt = (((1,), (1,)), ((), ()))
    acum_t = sum(lax.dot_general(p, tri, nt, preferred_element_type=F32) for p in _split3(a_t))

    edge = 0 if rev else q - 1
    total = acum[edge:edge + 1, :]
    w_state = dt * jnp.exp(total - acum)
    cd = jnp.exp(total)

    lane = lax.broadcasted_iota(jnp.int32, (q, LANES), 1)
    lo_half = lane < SSD_HEAD_DIM
    lo_row = lo_half[0:1, :]

    gsq = jnp.zeros((q, LANES), F32)
    for g in range(SSD_GROUPS):
        b_g = bcv[:, g * SSD_STATE:(g + 1) * SSD_STATE]
        c_g = bcv[:, SSD_BC + g * SSD_STATE:SSD_BC + (g + 1) * SSD_STATE]
        b_gb = b_g.astype(BF16)
        c_gb = c_g.astype(BF16)
        cb = lax.dot_general(c_gb, b_gb, nt, preferred_element_type=F32)
        gw = SSD_HPG * SSD_HEAD_DIM
        s_in = st_ref[:, g * gw:(g + 1) * gw].astype(BF16)
        yoff = jnp.dot(c_gb, s_in, preferred_element_type=F32)
        xw_parts = []
        for kp in range(SSD_HPG // 2):
            pair = g * (SSD_HPG // 2) + kp
            col = pair * LANES
            ms, ecols, wcols, cds = [], [], [], []
            for e in (2 * pair, 2 * pair + 1):
                colb = jnp.broadcast_to(acum[:, e:e + 1], (q, q))
                seg = colb - acum_t[e:e + 1, :]
                lmat = jnp.exp(jnp.where(mask, seg, -1e30))
                ms.append((cb * lmat * dt_t[e:e + 1, :]).astype(BF16))
                ecols.append(jnp.exp(colb))
                wcols.append(jnp.broadcast_to(w_state[:, e:e + 1], (q, LANES)))
                cds.append(jnp.broadcast_to(cd[:, e:e + 1], (1, LANES)))
            xp = xs[:, col:col + LANES]
            rhs = jnp.concatenate([jnp.where(lo_half, xp, 0.0).astype(BF16),
                                   jnp.where(lo_half, 0.0, xp).astype(BF16)], axis=0)
            lhs = jnp.concatenate(ms, axis=1)
            ydiag = jnp.dot(lhs, rhs, preferred_element_type=F32)
            y_pair = ydiag + jnp.where(lo_half, ecols[0], ecols[1]) * yoff[:, kp * LANES:(kp + 1) * LANES]
            xw_parts.append((xp * jnp.where(lo_half, wcols[0], wcols[1])).astype(BF16))
            cd_pair = jnp.where(lo_row, cds[0], cds[1])
            st_ref[:, col:col + LANES] = st_ref[:, col:col + LANES] * cd_pair
            if final:
                y_tot = y_pair + yf_ref[:, col:col + LANES] + dskip_ref[:, col:col + LANES] * xp
                zz = z_ref[:, col:col + LANES].astype(F32)
                gated = y_tot * (zz * _sigmoid(zz))
                gat_ref[:, col:col + LANES] = gated
                gsq = gsq + gated * gated
            else:
                y_ref[:, col:col + LANES] = y_pair
        xw = jnp.concatenate(xw_parts, axis=1)
        b_t = b_g.T.astype(BF16)
        st_ref[:, g * gw:(g + 1) * gw] += jnp.dot(b_t, xw, preferred_element_type=F32)

    if final:
        ms = jnp.sum(gsq, axis=-1, keepdims=True) * (1.0 / SSD_WIDTH)
        y_ref[...] = (gat_ref[...] * lax.rsqrt(ms + EPS) * nw_ref[...]).astype(BF16)

    @pl.when(i == nc - 1)
    def _():
        sfin_ref[...] = st_ref[...]


def _ssd_sweep(p, dtraw, s0, cwx, cbx, cwbc, cbbc, dtb, alog, *, rev, extra=None):
    bsz, s, _ = p.shape
    nc = s // SSD_CHUNK
    q = SSD_CHUNK
    hb = q // HALO
    final = extra is not None

    def cidx(i):
        return (nc - 1 - i) if rev else i

    def cur(col_block):
        return lambda b, i: (b, cidx(i), col_block)

    def prev(col_block):
        return lambda b, i: (b, jnp.maximum(cidx(i) * hb - 1, 0), col_block)

    def nxt(col_block):
        return lambda b, i: (b, jnp.minimum(cidx(i) * hb + hb, s // HALO - 1), col_block)

    const2 = lambda b, i: (0, 0)
    xw_, bw_ = SSD_WIDTH, 2 * SSD_BC
    in_specs = [pl.BlockSpec((None, q, xw_), cur(COL_XS // xw_)),
                pl.BlockSpec((None, HALO, xw_), prev(COL_XS // xw_)),
                pl.BlockSpec((None, HALO, xw_), nxt(COL_XS // xw_)),
                pl.BlockSpec((None, q, bw_), cur(COL_BC // bw_)),
                pl.BlockSpec((None, HALO, bw_), prev(COL_BC // bw_)),
                pl.BlockSpec((None, HALO, bw_), nxt(COL_BC // bw_)),
                pl.BlockSpec((None, q, LANES), cur(1 if rev else 0)),
                pl.BlockSpec((CONV_W, xw_), const2),
                pl.BlockSpec((1, xw_), const2),
                pl.BlockSpec((CONV_W, bw_), const2),
                pl.BlockSpec((1, bw_), const2),
                pl.BlockSpec((1, LANES), const2),
                pl.BlockSpec((1, LANES), const2),
                pl.BlockSpec((None, SSD_STATE, xw_), lambda b, i: (b, 0, 0))]
    args = [p, p, p, p, p, p, dtraw, cwx, cbx, cwbc, cbbc, dtb, alog, s0]
    scratch = [pltpu.VMEM((SSD_STATE, xw_), F32),
               pltpu.VMEM((q + 2 * HALO, xw_), F32),
               pltpu.VMEM((q + 2 * HALO, bw_), F32)]
    if final:
        yf, dskip, nw = extra
        in_specs += [pl.BlockSpec((None, q, xw_), cur(0)),
                     pl.BlockSpec((None, q, xw_), cur(COL_Z // xw_)),
                     pl.BlockSpec((1, xw_), const2),
                     pl.BlockSpec((1, xw_), const2)]
        args += [yf, p, dskip, nw]
        scratch.append(pltpu.VMEM((q, xw_), F32))
    y_dtype = BF16 if final else F32
    kern = functools.partial(_ssd_kernel, rev=rev, final=final, nc=nc)
    return pl.pallas_call(
        kern,
        out_shape=(jax.ShapeDtypeStruct((bsz, s, xw_), y_dtype),
                   jax.ShapeDtypeStruct((bsz, SSD_STATE, xw_), F32)),
        grid=(bsz, nc),
        in_specs=in_specs,
        out_specs=(pl.BlockSpec((None, q, xw_), cur(0)),
                   pl.BlockSpec((None, SSD_STATE, xw_), lambda b, i: (b, 0, 0))),
        scratch_shapes=scratch,
        compiler_params=_params(("arbitrary", "arbitrary")),
        name="ssd_" + ("bwd" if rev else "fwd") + ("_final" if final else ""),
    )(*args)


def _lru_kernel(*refs, rows, width, tb, emit):
    if emit:
        (xr_ref, yr_ref, cw_ref, cb_ref, wg_ref, bg_ref, lam_ref, h0_ref,
         rg_ref, fin_ref,
         xe_ref, af_ref, uf_ref, ab_ref, ub_ref, sh_ref, fl_ref, pe_ref, cinf_ref, cinb_ref) = refs
    else:
        (xr_ref, cw_ref, cb_ref, wg_ref, bg_ref, lam_ref, h0_ref,
         fin_ref,
         xe_ref, af_ref, uf_ref, ab_ref, ub_ref, sh_ref, fl_ref, pe_ref, cinf_ref, cinb_ref) = refs
    R, W = rows, width
    hd = LRU_HEAD_DIM
    rb = tb // W

    xe_ref[2:2 + R] = xr_ref[...].astype(F32).reshape(R, W, hd)
    zero8 = jnp.zeros((8, hd), F32)
    sh_ref[0:8, :] = zero8
    sh_ref[8 + W:16 + W, :] = zero8
    for src, dst, off in ((R - 2, 0, 7), (R - 1, 1, 7), (0, R + 2, 9)):
        sh_ref[8:8 + W, :] = xe_ref[2 + src]
        xe_ref[dst] = sh_ref[off:off + W, :]

    sp = _softplus(-lam_ref[...])
    cwv = cw_ref[...]
    cbv = cb_ref[...]
    wg = wg_ref[...]
    bg = bg_ref[...]

    def gate_body(blk, carry):
        r0 = blk * rb
        xc = cbv
        for k in range(CONV_W):
            xc = xc + cwv[k:k + 1, :] * xe_ref[pl.ds(r0 + k, rb)]
        xc2 = xc.reshape(rb * W, hd)
        pre = jnp.dot(xc2.astype(BF16), wg, preferred_element_type=F32) + bg
        for d, (a_ref, u_ref) in enumerate(((af_ref, uf_ref), (ab_ref, ub_ref))):
            r_gate = _sigmoid(pre[:, (2 * d) * hd:(2 * d + 1) * hd])
            i_gate = _sigmoid(pre[:, (2 * d + 1) * hd:(2 * d + 2) * hd])
            log_a = (-LRU_C * r_gate) * sp[d:d + 1, :]
            a = jnp.exp(log_a)
            u = jnp.sqrt((1.0 - a) * (1.0 + a)) * (i_gate * xc2)
            a_ref[pl.ds(r0, rb)] = a.reshape(rb, W, hd)
            u_ref[pl.ds(r0, rb)] = u.reshape(rb, W, hd)
        return carry

    lax.fori_loop(0, R // rb, gate_body, 0)

    zeros = jnp.zeros((W, hd), F32)
    ones = jnp.ones((W, hd), F32)

    def local_scan(a_ref, u_ref, reverse):
        def body(t, carry):
            h, p = carry
            r = (R - 1 - t) if reverse else t
            a = a_ref[r]
            h = a * h + u_ref[r]
            p = a * p
            u_ref[r] = h
            a_ref[r] = p
            return h, p
        return lax.fori_loop(0, R, body, (zeros, ones))

    def carry_chain(h_end, p_end, h0, cin_ref, reverse):
        fl_ref[...] = h_end
        pe_ref[...] = p_end

        def body(t, carry):
            cc = (W - 1 - t) if reverse else t
            cin_ref[pl.ds(cc, 1), :] = carry
            return fl_ref[pl.ds(cc, 1), :] + pe_ref[pl.ds(cc, 1), :] * carry
        return lax.fori_loop(0, W, body, h0)

    hf_end, pf_end = local_scan(af_ref, uf_ref, False)
    fin_f = carry_chain(hf_end, pf_end, h0_ref[0:1, :], cinf_ref, False)
    hb_end, pb_end = local_scan(ab_ref, ub_ref, True)
    fin_b = carry_chain(hb_end, pb_end, h0_ref[1:2, :], cinb_ref, True)
    fin_ref[0:1, :] = fin_f
    fin_ref[1:2, :] = fin_b

    if emit:
        cin_f = cinf_ref[...]
        cin_b = cinb_ref[...]

        def out_body(r, carry):
            h = (uf_ref[r] + af_ref[r] * cin_f) + (ub_ref[r] + ab_ref[r] * cin_b)
            row0 = pl.multiple_of(r * W, W)
            y = yr_ref[pl.ds(row0, W), :].astype(F32)
            rg_ref[pl.ds(row0, W), :] = (h * jax.nn.gelu(y)).astype(BF16)
            return carry

        lax.fori_loop(0, R, out_body, 0)


def _lru(xr_src, xr_col, yr_src, yr_col, cw, cb, wg, bg, lam, h0, *, rows, width, emit):
    bsz, s, _ = xr_src.shape
    hd = LRU_HEAD_DIM
    tb = min(512, s)
    kern = functools.partial(_lru_kernel, rows=rows, width=width, tb=tb, emit=emit)
    head = lambda b, h: (0, h)
    in_specs = [pl.BlockSpec((None, s, hd), lambda b, h: (b, 0, xr_col // hd + h))]
    args = [xr_src]
    if emit:
        in_specs.append(pl.BlockSpec((None, s, hd), lambda b, h: (b, 0, yr_col // hd + h)))
        args.append(yr_src)
    in_specs += [pl.BlockSpec((CONV_W, hd), head),
                 pl.BlockSpec((1, hd), head),
                 pl.BlockSpec((None, hd, 4 * hd), lambda b, h: (h, 0, 0)),
                 pl.BlockSpec((None, 1, 4 * hd), lambda b, h: (h, 0, 0)),
                 pl.BlockSpec((2, hd), head),
                 pl.BlockSpec((None, 2, hd), lambda b, h: (b, 0, h))]
    args += [cw, cb, wg, bg, lam, h0]
    out_shape = [jax.ShapeDtypeStruct((bsz, 2, LRU_WIDTH), F32)]
    out_specs = [pl.BlockSpec((None, 2, hd), lambda b, h: (b, 0, h))]
    if emit:
        out_shape.insert(0, jax.ShapeDtypeStruct((bsz, s, LRU_WIDTH), BF16))
        out_specs.insert(0, pl.BlockSpec((None, s, hd), lambda b, h: (b, 0, h)))
    big = pltpu.VMEM((rows, width, hd), F32)
    scratch = [pltpu.VMEM((rows + 3, width, hd), F32), big, big, big, big,
               pltpu.VMEM((width + 16, hd), F32),
               pltpu.VMEM((width, hd), F32), pltpu.VMEM((width, hd), F32),
               pltpu.VMEM((width, hd), F32), pltpu.VMEM((width, hd), F32)]
    return pl.pallas_call(
        kern,
        out_shape=tuple(out_shape),
        grid=(bsz, LRU_HEADS),
        in_specs=in_specs,
        out_specs=tuple(out_specs),
        scratch_shapes=scratch,
        compiler_params=_params(("arbitrary", "arbitrary")),
        name="lru" + ("" if emit else "_ctx"),
    )(*args)


def _merge_a_kernel(yn_ref, rg_ref, ws_ref, wl_ref, gs_ref, gr_ref, m_ref):
    o_s = jnp.dot(yn_ref[...], ws_ref[...], preferred_element_type=F32)
    o_r = jnp.dot(rg_ref[...], wl_ref[...], preferred_element_type=F32)
    m_ref[...] = (gs_ref[...].astype(F32) * o_s + gr_ref[...].astype(F32) * o_r).astype(BF16)


def _merge_a(yn, rg, ws, wl, p, tm, tn):
    bsz, s, d = yn.shape
    return pl.pallas_call(
        _merge_a_kernel,
        out_shape=jax.ShapeDtypeStruct((bsz, s, d), BF16),
        grid=(bsz, s // tm, d // tn),
        in_specs=[pl.BlockSpec((None, tm, d), lambda b, i, n: (b, i, 0)),
                  pl.BlockSpec((None, tm, d), lambda b, i, n: (b, i, 0)),
                  pl.BlockSpec((d, tn), lambda b, i, n: (0, n)),
                  pl.BlockSpec((d, tn), lambda b, i, n: (0, n)),
                  pl.BlockSpec((None, tm, tn), lambda b, i, n: (b, i, COL_GS // tn + n)),
                  pl.BlockSpec((None, tm, tn), lambda b, i, n: (b, i, COL_GR // tn + n))],
        out_specs=pl.BlockSpec((None, tm, tn), lambda b, i, n: (b, i, n)),
        compiler_params=_params(("arbitrary", "arbitrary", "arbitrary")),
        name="merge_a",
    )(yn, rg, ws, wl, p, p)


def _merge_b_kernel(m_ref, w_ref, x_ref, gm_ref, o_ref):
    mix = jnp.dot(m_ref[...], w_ref[...], preferred_element_type=F32)
    o_ref[...] = x_ref[...] + gm_ref[...] * mix


def _merge_b(m, w_o, x, gm, tm, tn):
    bsz, s, d = m.shape
    return pl.pallas_call(
        _merge_b_kernel,
        out_shape=jax.ShapeDtypeStruct((bsz, s, d), F32),
        grid=(bsz, s // tm, d // tn),
        in_specs=[pl.BlockSpec((None, tm, d), lambda b, i, n: (b, i, 0)),
                  pl.BlockSpec((d, tn), lambda b, i, n: (0, n)),
                  pl.BlockSpec((None, tm, tn), lambda b, i, n: (b, i, n)),
                  pl.BlockSpec((None, 1, tn), lambda b, i, n: (b, 0, n))],
        out_specs=pl.BlockSpec((None, tm, tn), lambda b, i, n: (b, i, n)),
        compiler_params=_params(("arbitrary", "arbitrary", "arbitrary")),
        name="merge_b",
    )(m, w_o, x, gm)


def _ffn_up_kernel(x_ref, sh_ref, sc_ref, g_ref, w1_ref, w3_ref, a_ref, hn_ref):
    @pl.when(pl.program_id(2) == 0)
    def _():
        hn_ref[...] = _norm_mod(x_ref[...], g_ref[...], sh_ref[...], sc_ref[...]).astype(BF16)

    h = hn_ref[...]
    gate = jnp.dot(h, w1_ref[...], preferred_element_type=F32)
    up = jnp.dot(h, w3_ref[...], preferred_element_type=F32)
    a_ref[...] = (gate * _sigmoid(gate) * up).astype(BF16)


def _ffn_up(x, shift, scale, g, w13, tm, tn):
    bsz, s, d = x.shape
    hid = w13.shape[1] // 2
    nb = hid // tn
    return pl.pallas_call(
        _ffn_up_kernel,
        out_shape=jax.ShapeDtypeStruct((bsz, s, hid), BF16),
        grid=(bsz, s // tm, nb),
        in_specs=[pl.BlockSpec((None, tm, d), lambda b, i, n: (b, i, 0)),
                  pl.BlockSpec((None, 1, d), lambda b, i, n: (b, 0, 0)),
                  pl.BlockSpec((None, 1, d), lambda b, i, n: (b, 0, 0)),
                  pl.BlockSpec((1, d), lambda b, i, n: (0, 0)),
                  pl.BlockSpec((d, tn), lambda b, i, n: (0, n)),
                  pl.BlockSpec((d, tn), lambda b, i, n: (0, nb + n))],
        out_specs=pl.BlockSpec((None, tm, tn), lambda b, i, n: (b, i, n)),
        scratch_shapes=[pltpu.VMEM((tm, d), BF16)],
        compiler_params=_params(("arbitrary", "arbitrary", "arbitrary")),
        name="ffn_up",
    )(x, shift, scale, g, w13, w13)


def _ffn_down_kernel(a_ref, w_ref, x_ref, gf_ref, fn_ref, o_ref, acc_ref, *, nk):
    k = pl.program_id(2)

    @pl.when(k == 0)
    def _():
        acc_ref[...] = jnp.zeros_like(acc_ref)

    acc_ref[...] += jnp.dot(a_ref[...], w_ref[...], preferred_element_type=F32)

    @pl.when(k == nk - 1)
    def _():
        x2 = x_ref[...] + gf_ref[...] * acc_ref[...]
        ms = jnp.mean(x2 * x2, axis=-1, keepdims=True)
        o_ref[...] = x2 * lax.rsqrt(ms + EPS) * fn_ref[...]


def _ffn_down(a, w2, x1, gf, fnorm, tm, tk):
    bsz, s, hid = a.shape
    d = w2.shape[1]
    nk = hid // tk
    kern = functools.partial(_ffn_down_kernel, nk=nk)
    return pl.pallas_call(
        kern,
        out_shape=jax.ShapeDtypeStruct((bsz, s, d), F32),
        grid=(bsz, s // tm, nk),
        in_specs=[pl.BlockSpec((None, tm, tk), lambda b, i, k: (b, i, k)),
                  pl.BlockSpec((tk, d), lambda b, i, k: (k, 0)),
                  pl.BlockSpec((None, tm, d), lambda b, i, k: (b, i, 0)),
                  pl.BlockSpec((None, 1, d), lambda b, i, k: (b, 0, 0)),
                  pl.BlockSpec((1, d), lambda b, i, k: (0, 0))],
        out_specs=pl.BlockSpec((None, tm, d), lambda b, i, k: (b, i, 0)),
        scratch_shapes=[pltpu.VMEM((tm, d), F32)],
        compiler_params=_params(("arbitrary", "arbitrary", "arbitrary")),
        name="ffn_down",
    )(a, w2, x1, gf, fnorm)


def _pad_lanes(v, n=LANES):
    return jnp.pad(v, (0, n - v.shape[0])).reshape(1, n)


def _layer(x, ctx, mod, mod_c, prm):
    bsz, seq, d = x.shape
    clen = ctx.shape[1]
    rows = seq // GRID_W

    sh_m, sc_m, g_m, sh_f, sc_f, g_f = [m.reshape(bsz, 1, d) for m in jnp.split(mod, N_MOD, axis=-1)]
    csh_m, csc_m = [jnp.broadcast_to(m.reshape(1, 1, d), (bsz, 1, d))
                    for m in jnp.split(mod_c, N_MOD, axis=-1)[:2]]

    w_in = prm["w_in"]
    o = 0
    parts = {}
    for name, size in (("z", SSD_WIDTH), ("xs", SSD_WIDTH), ("b", SSD_BC), ("c", SSD_BC),
                       ("dt", 2 * SSD_HEADS), ("xr", LRU_WIDTH), ("yr", LRU_WIDTH)):
        parts[name] = w_in[:, o:o + size]
        o += size
    w_cat = jnp.concatenate([parts["z"], parts["xs"], parts["xr"], parts["yr"], prm["w_gate"],
                             parts["b"], parts["c"]], axis=1).astype(BF16)
    b_cat = jnp.concatenate([jnp.zeros((COL_GS,), F32), prm["b_gate"],
                             jnp.zeros((NP - COL_BC,), F32)]).reshape(1, NP)
    w_dt = jnp.zeros((d, DT_COLS), F32)
    w_dt = w_dt.at[:, 0:SSD_HEADS].set(parts["dt"][:, :SSD_HEADS])
    w_dt = w_dt.at[:, LANES:LANES + SSD_HEADS].set(parts["dt"][:, SSD_HEADS:]).astype(BF16)
    g_mix = prm["norm_mix"].reshape(1, d)

    p_ctx, dt_ctx = _in_proj(ctx, csh_m, csc_m, g_mix, w_cat, b_cat, w_dt, tm=clen)
    p_lat, dt_lat = _in_proj(x, sh_m, sc_m, g_mix, w_cat, b_cat, w_dt, tm=512)

    cwx = prm["ssd_conv_w"][:, :SSD_WIDTH]
    cwbc = prm["ssd_conv_w"][:, SSD_WIDTH:]
    cbx = prm["ssd_conv_b"][:SSD_WIDTH].reshape(1, -1)
    cbbc = prm["ssd_conv_b"][SSD_WIDTH:].reshape(1, -1)
    dtb = [_pad_lanes(prm["ssd_dt_bias"][k]) for k in range(2)]
    alog = [_pad_lanes(prm["ssd_a_log"][k]) for k in range(2)]
    conv = (cwx, cbx, cwbc, cbbc)
    zero_state = jnp.zeros((bsz, SSD_STATE, SSD_WIDTH), F32)
    _, s_f = _ssd_sweep(p_ctx, dt_ctx, zero_state, *conv, dtb[0], alog[0], rev=False)
    _, s_b = _ssd_sweep(p_ctx, dt_ctx, zero_state, *conv, dtb[1], alog[1], rev=True)
    y_f, _ = _ssd_sweep(p_lat, dt_lat, s_f, *conv, dtb[0], alog[0], rev=False)
    dskip = jnp.repeat(prm["ssd_d"], SSD_HEAD_DIM).reshape(1, SSD_WIDTH)
    yn, _ = _ssd_sweep(p_lat, dt_lat, s_b, *conv, dtb[1], alog[1], rev=True,
                       extra=(y_f, dskip, prm["ssd_norm"].reshape(1, SSD_WIDTH)))

    hd = LRU_HEAD_DIM
    wg = jnp.concatenate([prm["lru_w_a"][0], prm["lru_w_x"][0],
                          prm["lru_w_a"][1], prm["lru_w_x"][1]], axis=-1).astype(BF16)
    bg = jnp.stack([prm["lru_b_a"][0].reshape(LRU_HEADS, hd), prm["lru_b_x"][0].reshape(LRU_HEADS, hd),
                    prm["lru_b_a"][1].reshape(LRU_HEADS, hd), prm["lru_b_x"][1].reshape(LRU_HEADS, hd)],
                   axis=1).reshape(LRU_HEADS, 1, 4 * hd)
    lcw = prm["lru_conv_w"]
    lcb = prm["lru_conv_b"].reshape(1, -1)
    lam = prm["lru_lambda"]
    cw_ = 8
    cr_ = clen // cw_
    xr_ctx = p_ctx[:, :, COL_XR:COL_XR + LRU_WIDTH]
    xr_ctx = xr_ctx.reshape(bsz, cw_, cr_, LRU_WIDTH).transpose(0, 2, 1, 3).reshape(bsz, clen, LRU_WIDTH)
    zero_h = jnp.zeros((bsz, 2, LRU_WIDTH), F32)
    (f_ctx,) = _lru(xr_ctx, 0, None, 0, lcw, lcb, wg, bg, lam, zero_h, rows=cr_, width=cw_, emit=False)
    rg, _ = _lru(p_lat, COL_XR, p_lat, COL_YR, lcw, lcb, wg, bg, lam, f_ctx,
                 rows=rows, width=GRID_W, emit=True)

    m = _merge_a(yn, rg, prm["w_out_ssd"].astype(BF16), prm["w_out_lru"].astype(BF16), p_lat, tm=512, tn=1024)
    x1 = _merge_b(m, prm["w_o"].astype(BF16), x, g_m, tm=512, tn=1024)

    act = _ffn_up(x1, sh_f, sc_f, prm["norm_ffn"].reshape(1, d), prm["ffn_w13"].astype(BF16), tm=512, tn=512)
    return _ffn_down(act, prm["ffn_w2"].astype(BF16), x1, g_f, prm["final_norm"].reshape(1, d), tm=512, tk=512)


def kernel(x, c, ctx, c_ctx, w_ada, b_ada, norm_mix, norm_ffn, w_in, ssd_conv_w, ssd_conv_b, ssd_dt_bias,
           ssd_a_log, ssd_d, ssd_norm, w_out_ssd, lru_conv_w, lru_conv_b, lru_w_a, lru_b_a, lru_w_x, lru_b_x,
           lru_lambda, w_out_lru, w_gate, b_gate, w_o, ffn_w13, ffn_w2, final_norm):
    assert w_ada.shape[0] == 1, "single-layer trunk"
    bsz = x.shape[0]
    cvec = jnp.zeros((8, D_MODEL), F32).at[:bsz].set(c).at[bsz].set(c_ctx)
    mod_all = _ada(cvec, w_ada[0], b_ada[0].reshape(1, -1))
    prm = dict(w_in=w_in[0], ssd_conv_w=ssd_conv_w[0], ssd_conv_b=ssd_conv_b[0], ssd_dt_bias=ssd_dt_bias[0],
               ssd_a_log=ssd_a_log[0], ssd_d=ssd_d[0], ssd_norm=ssd_norm[0], w_out_ssd=w_out_ssd[0],
               lru_conv_w=lru_conv_w[0], lru_conv_b=lru_conv_b[0], lru_w_a=lru_w_a[0], lru_b_a=lru_b_a[0],
               lru_w_x=lru_w_x[0], lru_b_x=lru_b_x[0], lru_lambda=lru_lambda[0], w_out_lru=w_out_lru[0],
               w_gate=w_gate[0], b_gate=b_gate[0], w_o=w_o[0], norm_mix=norm_mix[0], norm_ffn=norm_ffn[0],
               ffn_w13=ffn_w13[0], ffn_w2=ffn_w2[0], final_norm=final_norm)
    return _layer(x, ctx, mod_all[:bsz], mod_all[bsz], prm)
```

```python
import functools

import jax
import jax.numpy as jnp
from jax import lax
from jax.experimental import pallas as pl
from jax.experimental.pallas import tpu as pltpu

F32 = jnp.float32
BF16 = jnp.bfloat16

D_MODEL = 2048
GRID_W = 64
N_MOD = 6
SSD_HEAD_DIM = 64
SSD_HEADS = 32
SSD_WIDTH = SSD_HEADS * SSD_HEAD_DIM
SSD_GROUPS = 4
SSD_HPG = SSD_HEADS // SSD_GROUPS
SSD_STATE = 128
SSD_BC = SSD_GROUPS * SSD_STATE
SSD_CHUNK = 128
CONV_W = 4
CONV_PAD_LEFT = 2
LRU_WIDTH = D_MODEL
LRU_HEADS = 16
LRU_HEAD_DIM = LRU_WIDTH // LRU_HEADS
LRU_C = 8.0
EPS = 1e-6

LANES = 128
HALO = 16
VMEM_LIMIT = 56 * 1024 * 1024

COL_Z = 0
COL_XS = 2048
COL_XR = 4096
COL_YR = 6144
COL_GS = 8192
COL_GR = 10240
COL_BC = 12288
NP = 13312
DT_COLS = 256


TILES = {"in_proj": (1024, 1024), "merge_a": (1024, 512), "merge_b": (512, 2048),
         "ffn_up": (1024, 512), "ffn_down": (1024, 512)}


def _params(sem, vmem=VMEM_LIMIT):
    return pltpu.CompilerParams(dimension_semantics=sem, vmem_limit_bytes=vmem)


LOG2E = 1.4426950408889634


def _sigmoid(x):
    return 1.0 / (1.0 + jnp.exp2(x * (-LOG2E)))


def _softplus(x):
    return jnp.maximum(x, 0.0) + jnp.log1p(jnp.exp(-jnp.abs(x)))


def _norm_mod(x, g, shift, scale):
    xf = x.astype(F32)
    ms = jnp.mean(xf * xf, axis=-1, keepdims=True)
    hn = xf * lax.rsqrt(ms + EPS) * g
    return hn * (1.0 + scale) + shift


def _ada_kernel(c_ref, w_ref, b_ref, o_ref):
    c = c_ref[...]
    s = (c * _sigmoid(c)).astype(BF16)
    o_ref[...] = jnp.dot(s, w_ref[...].astype(BF16), preferred_element_type=F32) + b_ref[...]


def _ada(cvec, w, b):
    m, d = cvec.shape
    n = w.shape[1]
    tn = 1024
    return pl.pallas_call(
        _ada_kernel,
        out_shape=jax.ShapeDtypeStruct((m, n), F32),
        grid=(n // tn,),
        in_specs=[pl.BlockSpec((m, d), lambda j: (0, 0)),
                  pl.BlockSpec((d, tn), lambda j: (0, j)),
                  pl.BlockSpec((1, tn), lambda j: (0, j))],
        out_specs=pl.BlockSpec((m, tn), lambda j: (0, j)),
        compiler_params=_params(("arbitrary",)),
        name="ada",
    )(cvec, w, b)


def _in_proj_kernel(x_ref, sh_ref, sc_ref, g_ref, w_ref, b_ref, wdt_ref, p_ref, dt_ref, hn_ref,
                    *, gate_lo, gate_hi):
    n = pl.program_id(2)

    @pl.when(n == 0)
    def _():
        h = _norm_mod(x_ref[...], g_ref[...], sh_ref[...], sc_ref[...]).astype(BF16)
        hn_ref[...] = h
        dt_ref[...] = jnp.dot(h, wdt_ref[...], preferred_element_type=F32)

    acc = jnp.dot(hn_ref[...], w_ref[...], preferred_element_type=F32) + b_ref[...]
    is_gate = jnp.logical_and(n >= gate_lo, n < gate_hi)

    @pl.when(is_gate)
    def _():
        p_ref[...] = _sigmoid(acc).astype(BF16)

    @pl.when(jnp.logical_not(is_gate))
    def _():
        p_ref[...] = acc.astype(BF16)


def _in_proj(x, shift, scale, g, w_cat, b_cat, w_dt, tm):
    bsz, s, d = x.shape
    tn = 1024
    kern = functools.partial(_in_proj_kernel, gate_lo=COL_GS // tn, gate_hi=COL_BC // tn)
    return pl.pallas_call(
        kern,
        out_shape=(jax.ShapeDtypeStruct((bsz, s, NP), BF16),
                   jax.ShapeDtypeStruct((bsz, s, DT_COLS), F32)),
        grid=(bsz, s // tm, NP // tn),
        in_specs=[pl.BlockSpec((None, tm, d), lambda b, i, n: (b, i, 0)),
                  pl.BlockSpec((None, 1, d), lambda b, i, n: (b, 0, 0)),
                  pl.BlockSpec((None, 1, d), lambda b, i, n: (b, 0, 0)),
                  pl.BlockSpec((1, d), lambda b, i, n: (0, 0)),
                  pl.BlockSpec((d, tn), lambda b, i, n: (0, n)),
                  pl.BlockSpec((1, tn), lambda b, i, n: (0, n)),
                  pl.BlockSpec((d, DT_COLS), lambda b, i, n: (0, 0))],
        out_specs=(pl.BlockSpec((None, tm, tn), lambda b, i, n: (b, i, n)),
                   pl.BlockSpec((None, tm, DT_COLS), lambda b, i, n: (b, i, 0))),
        scratch_shapes=[pltpu.VMEM((tm, d), BF16)],
        compiler_params=_params(("arbitrary", "arbitrary", "arbitrary")),
        name="in_proj",
    )(x, shift, scale, g, w_cat, b_cat, w_dt)


def _split3(v):
    v1 = v.astype(BF16)
    r1 = v - v1.astype(F32)
    v2 = r1.astype(BF16)
    r2 = r1 - v2.astype(F32)
    return v1, v2, r2.astype(BF16)


def _ssd_kernel(*refs, rev, final, nc):
    if final:
        (xs_ref, xsp_ref, xsn_ref, bc_ref, bcp_ref, bcn_ref, dt_ref,
         cwx_ref, cbx_ref, cwbc_ref, cbbc_ref, dtb_ref, alog_ref, s0_ref,
         yf_ref, z_ref, dskip_ref, nw_ref,
         y_ref, sfin_ref, st_ref, gat_ref) = refs
    else:
        (xs_ref, xsp_ref, xsn_ref, bc_ref, bcp_ref, bcn_ref, dt_ref,
         cwx_ref, cbx_ref, cwbc_ref, cbbc_ref, dtb_ref, alog_ref, s0_ref,
         y_ref, sfin_ref, st_ref) = refs

    i = pl.program_id(1)
    c = (nc - 1 - i) if rev else i
    q = SSD_CHUNK

    @pl.when(i == 0)
    def _():
        st_ref[...] = s0_ref[...]

    has_prev = c > 0
    has_next = c < nc - 1

    tt = lax.broadcasted_iota(jnp.int32, (q, CONV_W * q), 0)
    rr = lax.broadcasted_iota(jnp.int32, (q, CONV_W * q), 1)
    shift_cur = jnp.where((rr % q) == tt + (rr // q) - CONV_PAD_LEFT, 1.0, 0.0).astype(BF16)
    t16 = lax.broadcasted_iota(jnp.int32, (HALO, 2 * HALO), 0)
    r16 = lax.broadcasted_iota(jnp.int32, (HALO, 2 * HALO), 1)
    top_sel = ((r16 == HALO - 2 + t16) & (t16 < 2)) | ((r16 == 2 * HALO - 1) & (t16 == 0))
    shift_top = jnp.where(top_sel, 1.0, 0.0).astype(BF16)
    tb16 = lax.broadcasted_iota(jnp.int32, (HALO, HALO), 0)
    rb16 = lax.broadcasted_iota(jnp.int32, (HALO, HALO), 1)
    shift_bot = jnp.where((tb16 == HALO - 1) & (rb16 == 0), 1.0, 0.0).astype(BF16)

    def conv_silu(cur_ref, prev_ref, next_ref, w_ref, b_ref):
        wb = [w_ref[k:k + 1, :].astype(BF16) for k in range(CONV_W)]
        cur = cur_ref[...]
        taps = jnp.concatenate([cur * wb[k] for k in range(CONV_W)], axis=0)
        out = jnp.dot(shift_cur, taps, preferred_element_type=F32) + b_ref[...]
        prev = jnp.where(has_prev, prev_ref[...], jnp.zeros_like(prev_ref))
        top = jnp.concatenate([prev * wb[0], prev * wb[1]], axis=0)
        nxt_rows = jnp.where(has_next, next_ref[...], jnp.zeros_like(next_ref))
        bot = nxt_rows * wb[CONV_W - 1]
        head = out[0:HALO] + jnp.dot(shift_top, top, preferred_element_type=F32)
        tail = out[q - HALO:q] + jnp.dot(shift_bot, bot, preferred_element_type=F32)
        out = jnp.concatenate([head, out[HALO:q - HALO], tail], axis=0)
        return out * _sigmoid(out)

    xs = conv_silu(xs_ref, xsp_ref, xsn_ref, cwx_ref, cbx_ref)
    bcv = conv_silu(bc_ref, bcp_ref, bcn_ref, cwbc_ref, cbbc_ref)

    dt = _softplus(dt_ref[...] + dtb_ref[...])
    a = dt * (-jnp.exp(alog_ref[...]))

    ii = lax.broadcasted_iota(jnp.int32, (q, q), 0)
    jj = lax.broadcasted_iota(jnp.int32, (q, q), 1)
    mask = (jj >= ii) if rev else (jj <= ii)
    tri = jnp.where(mask, 1.0, 0.0).astype(BF16)

    acum = sum(jnp.dot(tri, p, preferred_element_type=F32) for p in _split3(a))
    a_t = a.T
    dt_t = dt.T
    nt = (((1,), (1,)), ((), ()))
    acum_t = sum(lax.dot_general(p, tri, nt, preferred_element_type=F32) for p in _split3(a_t))

    edge = 0 if rev else q - 1
    total = acum[edge:edge + 1, :]
    w_state = dt * jnp.exp(total - acum)
    cd = jnp.exp(total)
    acum2 = acum * LOG2E
    acum2_t = acum_t * LOG2E

    lane = lax.broadcasted_iota(jnp.int32, (q, LANES), 1)
    lo_half = lane < SSD_HEAD_DIM
    lo_row = lo_half[0:1, :]

    gsq = jnp.zeros((q, LANES), F32)
    for g in range(SSD_GROUPS):
        b_g = bcv[:, g * SSD_STATE:(g + 1) * SSD_STATE]
        c_g = bcv[:, SSD_BC + g * SSD_STATE:SSD_BC + (g + 1) * SSD_STATE]
        b_gb = b_g.astype(BF16)
        c_gb = c_g.astype(BF16)
        cb = lax.dot_general(c_gb, b_gb, nt, preferred_element_type=F32)
        gw = SSD_HPG * SSD_HEAD_DIM
        s_in = st_ref[:, g * gw:(g + 1) * gw].astype(BF16)
        yoff = jnp.dot(c_gb, s_in, preferred_element_type=F32)
        xw_parts = []
        for kp in range(SSD_HPG // 2):
            pair = g * (SSD_HPG // 2) + kp
            col = pair * LANES
            ms, ecols, wcols, cds = [], [], [], []
            for e in (2 * pair, 2 * pair + 1):
                colb = jnp.broadcast_to(acum2[:, e:e + 1], (q, q))
                seg = colb - acum2_t[e:e + 1, :]
                lmat = jnp.exp2(jnp.where(mask, seg, -1e30))
                ms.append((cb * lmat * dt_t[e:e + 1, :]).astype(BF16))
                ecols.append(jnp.exp2(colb))
                wcols.append(jnp.broadcast_to(w_state[:, e:e + 1], (q, LANES)))
                cds.append(jnp.broadcast_to(cd[:, e:e + 1], (1, LANES)))
            xp = xs[:, col:col + LANES]
            rhs = jnp.concatenate([jnp.where(lo_half, xp, 0.0).astype(BF16),
                                   jnp.where(lo_half, 0.0, xp).astype(BF16)], axis=0)
            lhs = jnp.concatenate(ms, axis=1)
            ydiag = jnp.dot(lhs, rhs, preferred_element_type=F32)
            y_pair = ydiag + jnp.where(lo_half, ecols[0], ecols[1]) * yoff[:, kp * LANES:(kp + 1) * LANES]
            xw_parts.append((xp * jnp.where(lo_half, wcols[0], wcols[1])).astype(BF16))
            cd_pair = jnp.where(lo_row, cds[0], cds[1])
            st_ref[:, col:col + LANES] = st_ref[:, col:col + LANES] * cd_pair
            if final:
                y_tot = y_pair + yf_ref[:, col:col + LANES] + dskip_ref[:, col:col + LANES] * xp
                zz = z_ref[:, col:col + LANES].astype(F32)
                gated = y_tot * (zz * _sigmoid(zz))
                gat_ref[:, col:col + LANES] = gated
                gsq = gsq + gated * gated
            else:
                y_ref[:, col:col + LANES] = y_pair
        xw = jnp.concatenate(xw_parts, axis=1)
        b_t = b_g.T.astype(BF16)
        st_ref[:, g * gw:(g + 1) * gw] += jnp.dot(b_t, xw, preferred_element_type=F32)

    if final:
        ms = jnp.sum(gsq, axis=-1, keepdims=True) * (1.0 / SSD_WIDTH)
        y_ref[...] = (gat_ref[...] * lax.rsqrt(ms + EPS) * nw_ref[...]).astype(BF16)

    @pl.when(i == nc - 1)
    def _():
        sfin_ref[...] = st_ref[...]


def _ssd_sweep(p, dtraw, s0, cwx, cbx, cwbc, cbbc, dtb, alog, *, rev, extra=None):
    bsz, s, _ = p.shape
    nc = s // SSD_CHUNK
    q = SSD_CHUNK
    hb = q // HALO
    final = extra is not None

    def cidx(i):
        return (nc - 1 - i) if rev else i

    def cur(col_block):
        return lambda b, i: (b, cidx(i), col_block)

    def prev(col_block):
        return lambda b, i: (b, jnp.maximum(cidx(i) * hb - 1, 0), col_block)

    def nxt(col_block):
        return lambda b, i: (b, jnp.minimum(cidx(i) * hb + hb, s // HALO - 1), col_block)

    const2 = lambda b, i: (0, 0)
    xw_, bw_ = SSD_WIDTH, 2 * SSD_BC
    in_specs = [pl.BlockSpec((None, q, xw_), cur(COL_XS // xw_)),
                pl.BlockSpec((None, HALO, xw_), prev(COL_XS // xw_)),
                pl.BlockSpec((None, HALO, xw_), nxt(COL_XS // xw_)),
                pl.BlockSpec((None, q, bw_), cur(COL_BC // bw_)),
                pl.BlockSpec((None, HALO, bw_), prev(COL_BC // bw_)),
                pl.BlockSpec((None, HALO, bw_), nxt(COL_BC // bw_)),
                pl.BlockSpec((None, q, LANES), cur(1 if rev else 0)),
                pl.BlockSpec((CONV_W, xw_), const2),
                pl.BlockSpec((1, xw_), const2),
                pl.BlockSpec((CONV_W, bw_), const2),
                pl.BlockSpec((1, bw_), const2),
                pl.BlockSpec((1, LANES), const2),
                pl.BlockSpec((1, LANES), const2),
                pl.BlockSpec((None, SSD_STATE, xw_), lambda b, i: (b, 0, 0))]
    args = [p, p, p, p, p, p, dtraw, cwx, cbx, cwbc, cbbc, dtb, alog, s0]
    scratch = [pltpu.VMEM((SSD_STATE, xw_), F32)]
    if final:
        yf, dskip, nw = extra
        in_specs += [pl.BlockSpec((None, q, xw_), cur(0)),
                     pl.BlockSpec((None, q, xw_), cur(COL_Z // xw_)),
                     pl.BlockSpec((1, xw_), const2),
                     pl.BlockSpec((1, xw_), const2)]
        args += [yf, p, dskip, nw]
        scratch.append(pltpu.VMEM((q, xw_), F32))
    y_dtype = BF16 if final else F32
    kern = functools.partial(_ssd_kernel, rev=rev, final=final, nc=nc)
    return pl.pallas_call(
        kern,
        out_shape=(jax.ShapeDtypeStruct((bsz, s, xw_), y_dtype),
                   jax.ShapeDtypeStruct((bsz, SSD_STATE, xw_), F32)),
        grid=(bsz, nc),
        in_specs=in_specs,
        out_specs=(pl.BlockSpec((None, q, xw_), cur(0)),
                   pl.BlockSpec((None, SSD_STATE, xw_), lambda b, i: (b, 0, 0))),
        scratch_shapes=scratch,
        compiler_params=_params(("arbitrary", "arbitrary")),
        name="ssd_" + ("bwd" if rev else "fwd") + ("_final" if final else ""),
    )(*args)


def _lru_kernel(*refs, rows, width, tb, emit):
    if emit:
        (xr_ref, yr_ref, cw_ref, cb_ref, wg_ref, bg_ref, lam_ref, h0_ref,
         rg_ref, fin_ref,
         xe_ref, af_ref, uf_ref, ab_ref, ub_ref, sh_ref, fl_ref, pe_ref, cinf_ref, cinb_ref) = refs
    else:
        (xr_ref, cw_ref, cb_ref, wg_ref, bg_ref, lam_ref, h0_ref,
         fin_ref,
         xe_ref, af_ref, uf_ref, ab_ref, ub_ref, sh_ref, fl_ref, pe_ref, cinf_ref, cinb_ref) = refs
    R, W = rows, width
    hd = LRU_HEAD_DIM
    rb = tb // W

    xe_ref[2:2 + R] = xr_ref[...].astype(F32).reshape(R, W, hd)
    zero8 = jnp.zeros((8, hd), F32)
    sh_ref[0:8, :] = zero8
    sh_ref[8 + W:16 + W, :] = zero8
    for src, dst, off in ((R - 2, 0, 7), (R - 1, 1, 7), (0, R + 2, 9)):
        sh_ref[8:8 + W, :] = xe_ref[2 + src]
        xe_ref[dst] = sh_ref[off:off + W, :]

    a_scale = _softplus(-lam_ref[...]) * (-LRU_C * LOG2E)
    cwv = cw_ref[...]
    cbv = cb_ref[...]
    wg = wg_ref[...]
    bg = bg_ref[...]

    def gate_body(blk, carry):
        r0 = blk * rb
        xc = cbv
        for k in range(CONV_W):
            xc = xc + cwv[k:k + 1, :] * xe_ref[pl.ds(r0 + k, rb)]
        xc2 = xc.reshape(rb * W, hd)
        pre = jnp.dot(xc2.astype(BF16), wg, preferred_element_type=F32) + bg
        for d, (a_ref, u_ref) in enumerate(((af_ref, uf_ref), (ab_ref, ub_ref))):
            r_gate = _sigmoid(pre[:, (2 * d) * hd:(2 * d + 1) * hd])
            i_gate = _sigmoid(pre[:, (2 * d + 1) * hd:(2 * d + 2) * hd])
            a = jnp.exp2(r_gate * a_scale[d:d + 1, :])
            v = (1.0 - a) * (1.0 + a)
            u = (v * lax.rsqrt(jnp.maximum(v, 1e-37))) * (i_gate * xc2)
            a_ref[pl.ds(r0, rb)] = a.reshape(rb, W, hd)
            u_ref[pl.ds(r0, rb)] = u.reshape(rb, W, hd)
        return carry

    lax.fori_loop(0, R // rb, gate_body, 0)

    zeros = jnp.zeros((W, hd), F32)
    ones = jnp.ones((W, hd), F32)

    def scan_step(a_ref, u_ref, r, h, p):
        a = a_ref[r]
        h = a * h + u_ref[r]
        p = a * p
        u_ref[r] = h
        a_ref[r] = p
        return h, p

    def scan_body(t, carry):
        hf, pf, hb, pb = carry
        hf, pf = scan_step(af_ref, uf_ref, t, hf, pf)
        hb, pb = scan_step(ab_ref, ub_ref, R - 1 - t, hb, pb)
        return hf, pf, hb, pb

    def carry_chain(h_end, p_end, h0, cin_ref, reverse):
        fl_ref[...] = h_end
        pe_ref[...] = p_end

        def body(t, carry):
            cc = (W - 1 - t) if reverse else t
            cin_ref[pl.ds(cc, 1), :] = carry
            return fl_ref[pl.ds(cc, 1), :] + pe_ref[pl.ds(cc, 1), :] * carry
        return lax.fori_loop(0, W, body, h0)

    hf_end, pf_end, hb_end, pb_end = lax.fori_loop(0, R, scan_body, (zeros, ones, zeros, ones), unroll=4)
    fin_f = carry_chain(hf_end, pf_end, h0_ref[0:1, :], cinf_ref, False)
    fin_b = carry_chain(hb_end, pb_end, h0_ref[1:2, :], cinb_ref, True)
    fin_ref[0:1, :] = fin_f
    fin_ref[1:2, :] = fin_b

    if emit:
        cin_f = cinf_ref[...]
        cin_b = cinb_ref[...]

        def out_body(r, carry):
            h = (uf_ref[r] + af_ref[r] * cin_f) + (ub_ref[r] + ab_ref[r] * cin_b)
            row0 = pl.multiple_of(r * W, W)
            y = yr_ref[pl.ds(row0, W), :].astype(F32)
            rg_ref[pl.ds(row0, W), :] = (h * jax.nn.gelu(y)).astype(BF16)
            return carry

        lax.fori_loop(0, R, out_body, 0)


def _lru(xr_src, xr_col, yr_src, yr_col, cw, cb, wg, bg, lam, h0, *, rows, width, emit):
    bsz, s, _ = xr_src.shape
    hd = LRU_HEAD_DIM
    tb = min(512, s)
    kern = functools.partial(_lru_kernel, rows=rows, width=width, tb=tb, emit=emit)
    head = lambda b, h: (0, h)
    in_specs = [pl.BlockSpec((None, s, hd), lambda b, h: (b, 0, xr_col // hd + h))]
    args = [xr_src]
    if emit:
        in_specs.append(pl.BlockSpec((None, s, hd), lambda b, h: (b, 0, yr_col // hd + h)))
        args.append(yr_src)
    in_specs += [pl.BlockSpec((CONV_W, hd), head),
                 pl.BlockSpec((1, hd), head),
                 pl.BlockSpec((None, hd, 4 * hd), lambda b, h: (h, 0, 0)),
                 pl.BlockSpec((None, 1, 4 * hd), lambda b, h: (h, 0, 0)),
                 pl.BlockSpec((2, hd), head),
                 pl.BlockSpec((None, 2, hd), lambda b, h: (b, 0, h))]
    args += [cw, cb, wg, bg, lam, h0]
    out_shape = [jax.ShapeDtypeStruct((bsz, 2, LRU_WIDTH), F32)]
    out_specs = [pl.BlockSpec((None, 2, hd), lambda b, h: (b, 0, h))]
    if emit:
        out_shape.insert(0, jax.ShapeDtypeStruct((bsz, s, LRU_WIDTH), BF16))
        out_specs.insert(0, pl.BlockSpec((None, s, hd), lambda b, h: (b, 0, h)))
    big = pltpu.VMEM((rows, width, hd), F32)
    scratch = [pltpu.VMEM((rows + 3, width, hd), F32), big, big, big, big,
               pltpu.VMEM((width + 16, hd), F32),
               pltpu.VMEM((width, hd), F32), pltpu.VMEM((width, hd), F32),
               pltpu.VMEM((width, hd), F32), pltpu.VMEM((width, hd), F32)]
    return pl.pallas_call(
        kern,
        out_shape=tuple(out_shape),
        grid=(bsz, LRU_HEADS),
        in_specs=in_specs,
        out_specs=tuple(out_specs),
        scratch_shapes=scratch,
        compiler_params=_params(("arbitrary", "arbitrary")),
        name="lru" + ("" if emit else "_ctx"),
    )(*args)


def _merge_a_kernel(yn_ref, rg_ref, ws_ref, wl_ref, gs_ref, gr_ref, m_ref):
    o_s = jnp.dot(yn_ref[...], ws_ref[...], preferred_element_type=F32)
    o_r = jnp.dot(rg_ref[...], wl_ref[...], preferred_element_type=F32)
    m_ref[...] = (gs_ref[...].astype(F32) * o_s + gr_ref[...].astype(F32) * o_r).astype(BF16)


def _merge_a(yn, rg, ws, wl, p, tm, tn):
    bsz, s, d = yn.shape
    return pl.pallas_call(
        _merge_a_kernel,
        out_shape=jax.ShapeDtypeStruct((bsz, s, d), BF16),
        grid=(bsz, s // tm, d // tn),
        in_specs=[pl.BlockSpec((None, tm, d), lambda b, i, n: (b, i, 0)),
                  pl.BlockSpec((None, tm, d), lambda b, i, n: (b, i, 0)),
                  pl.BlockSpec((d, tn), lambda b, i, n: (0, n)),
                  pl.BlockSpec((d, tn), lambda b, i, n: (0, n)),
                  pl.BlockSpec((None, tm, tn), lambda b, i, n: (b, i, COL_GS // tn + n)),
                  pl.BlockSpec((None, tm, tn), lambda b, i, n: (b, i, COL_GR // tn + n))],
        out_specs=pl.BlockSpec((None, tm, tn), lambda b, i, n: (b, i, n)),
        compiler_params=_params(("arbitrary", "arbitrary", "arbitrary")),
        name="merge_a",
    )(yn, rg, ws, wl, p, p)


def _merge_b_kernel(m_ref, w_ref, x_ref, gm_ref, o_ref):
    mix = jnp.dot(m_ref[...], w_ref[...], preferred_element_type=F32)
    o_ref[...] = x_ref[...] + gm_ref[...] * mix


def _merge_b(m, w_o, x, gm, tm, tn):
    bsz, s, d = m.shape
    return pl.pallas_call(
        _merge_b_kernel,
        out_shape=jax.ShapeDtypeStruct((bsz, s, d), F32),
        grid=(bsz, s // tm, d // tn),
        in_specs=[pl.BlockSpec((None, tm, d), lambda b, i, n: (b, i, 0)),
                  pl.BlockSpec((d, tn), lambda b, i, n: (0, n)),
                  pl.BlockSpec((None, tm, tn), lambda b, i, n: (b, i, n)),
                  pl.BlockSpec((None, 1, tn), lambda b, i, n: (b, 0, n))],
        out_specs=pl.BlockSpec((None, tm, tn), lambda b, i, n: (b, i, n)),
        compiler_params=_params(("arbitrary", "arbitrary", "arbitrary")),
        name="merge_b",
    )(m, w_o, x, gm)


def _ffn_up_kernel(x_ref, sh_ref, sc_ref, g_ref, w1_ref, w3_ref, a_ref, hn_ref):
    @pl.when(pl.program_id(2) == 0)
    def _():
        hn_ref[...] = _norm_mod(x_ref[...], g_ref[...], sh_ref[...], sc_ref[...]).astype(BF16)

    h = hn_ref[...]
    gate = jnp.dot(h, w1_ref[...], preferred_element_type=F32)
    up = jnp.dot(h, w3_ref[...], preferred_element_type=F32)
    a_ref[...] = (gate * _sigmoid(gate) * up).astype(BF16)


def _ffn_up(x, shift, scale, g, w13, tm, tn):
    bsz, s, d = x.shape
    hid = w13.shape[1] // 2
    nb = hid // tn
    return pl.pallas_call(
        _ffn_up_kernel,
        out_shape=jax.ShapeDtypeStruct((bsz, s, hid), BF16),
        grid=(bsz, s // tm, nb),
        in_specs=[pl.BlockSpec((None, tm, d), lambda b, i, n: (b, i, 0)),
                  pl.BlockSpec((None, 1, d), lambda b, i, n: (b, 0, 0)),
                  pl.BlockSpec((None, 1, d), lambda b, i, n: (b, 0, 0)),
                  pl.BlockSpec((1, d), lambda b, i, n: (0, 0)),
                  pl.BlockSpec((d, tn), lambda b, i, n: (0, n)),
                  pl.BlockSpec((d, tn), lambda b, i, n: (0, nb + n))],
        out_specs=pl.BlockSpec((None, tm, tn), lambda b, i, n: (b, i, n)),
        scratch_shapes=[pltpu.VMEM((tm, d), BF16)],
        compiler_params=_params(("arbitrary", "arbitrary", "arbitrary")),
        name="ffn_up",
    )(x, shift, scale, g, w13, w13)


def _ffn_down_kernel(a_ref, w_ref, x_ref, gf_ref, fn_ref, o_ref, *, nk):
    k = pl.program_id(2)

    @pl.when(k == 0)
    def _():
        o_ref[...] = jnp.zeros_like(o_ref)

    o_ref[...] += jnp.dot(a_ref[...], w_ref[...], preferred_element_type=F32)

    @pl.when(k == nk - 1)
    def _():
        x2 = x_ref[...] + gf_ref[...] * o_ref[...]
        ms = jnp.mean(x2 * x2, axis=-1, keepdims=True)
        o_ref[...] = x2 * lax.rsqrt(ms + EPS) * fn_ref[...]


def _ffn_down(a, w2, x1, gf, fnorm, tm, tk):
    bsz, s, hid = a.shape
    d = w2.shape[1]
    nk = hid // tk
    kern = functools.partial(_ffn_down_kernel, nk=nk)
    return pl.pallas_call(
        kern,
        out_shape=jax.ShapeDtypeStruct((bsz, s, d), F32),
        grid=(bsz, s // tm, nk),
        in_specs=[pl.BlockSpec((None, tm, tk), lambda b, i, k: (b, i, k)),
                  pl.BlockSpec((tk, d), lambda b, i, k: (k, 0)),
                  pl.BlockSpec((None, tm, d), lambda b, i, k: (b, i, 0)),
                  pl.BlockSpec((None, 1, d), lambda b, i, k: (b, 0, 0)),
                  pl.BlockSpec((1, d), lambda b, i, k: (0, 0))],
        out_specs=pl.BlockSpec((None, tm, d), lambda b, i, k: (b, i, 0)),
        compiler_params=_params(("arbitrary", "arbitrary", "arbitrary")),
        name="ffn_down",
    )(a, w2, x1, gf, fnorm)


def _pad_lanes(v, n=LANES):
    return jnp.pad(v, (0, n - v.shape[0])).reshape(1, n)


def _layer(x, ctx, mod, mod_c, prm):
    bsz, seq, d = x.shape
    clen = ctx.shape[1]
    rows = seq // GRID_W

    sh_m, sc_m, g_m, sh_f, sc_f, g_f = [m.reshape(bsz, 1, d) for m in jnp.split(mod, N_MOD, axis=-1)]
    csh_m, csc_m = [jnp.broadcast_to(m.reshape(1, 1, d), (bsz, 1, d))
                    for m in jnp.split(mod_c, N_MOD, axis=-1)[:2]]

    w_in = prm["w_in"]
    o = 0
    parts = {}
    for name, size in (("z", SSD_WIDTH), ("xs", SSD_WIDTH), ("b", SSD_BC), ("c", SSD_BC),
                       ("dt", 2 * SSD_HEADS), ("xr", LRU_WIDTH), ("yr", LRU_WIDTH)):
        parts[name] = w_in[:, o:o + size]
        o += size
    w_cat = jnp.concatenate([parts["z"], parts["xs"], parts["xr"], parts["yr"], prm["w_gate"],
                             parts["b"], parts["c"]], axis=1).astype(BF16)
    b_cat = jnp.concatenate([jnp.zeros((COL_GS,), F32), prm["b_gate"],
                             jnp.zeros((NP - COL_BC,), F32)]).reshape(1, NP)
    w_dt = jnp.zeros((d, DT_COLS), F32)
    w_dt = w_dt.at[:, 0:SSD_HEADS].set(parts["dt"][:, :SSD_HEADS])
    w_dt = w_dt.at[:, LANES:LANES + SSD_HEADS].set(parts["dt"][:, SSD_HEADS:]).astype(BF16)
    g_mix = prm["norm_mix"].reshape(1, d)

    p_ctx, dt_ctx = _in_proj(ctx, csh_m, csc_m, g_mix, w_cat, b_cat, w_dt, tm=clen)
    p_lat, dt_lat = _in_proj(x, sh_m, sc_m, g_mix, w_cat, b_cat, w_dt, tm=TILES["in_proj"][0])

    cwx = prm["ssd_conv_w"][:, :SSD_WIDTH]
    cwbc = prm["ssd_conv_w"][:, SSD_WIDTH:]
    cbx = prm["ssd_conv_b"][:SSD_WIDTH].reshape(1, -1)
    cbbc = prm["ssd_conv_b"][SSD_WIDTH:].reshape(1, -1)
    dtb = [_pad_lanes(prm["ssd_dt_bias"][k]) for k in range(2)]
    alog = [_pad_lanes(prm["ssd_a_log"][k]) for k in range(2)]
    conv = (cwx, cbx, cwbc, cbbc)
    zero_state = jnp.zeros((bsz, SSD_STATE, SSD_WIDTH), F32)
    _, s_f = _ssd_sweep(p_ctx, dt_ctx, zero_state, *conv, dtb[0], alog[0], rev=False)
    _, s_b = _ssd_sweep(p_ctx, dt_ctx, zero_state, *conv, dtb[1], alog[1], rev=True)
    y_f, _ = _ssd_sweep(p_lat, dt_lat, s_f, *conv, dtb[0], alog[0], rev=False)
    dskip = jnp.repeat(prm["ssd_d"], SSD_HEAD_DIM).reshape(1, SSD_WIDTH)
    yn, _ = _ssd_sweep(p_lat, dt_lat, s_b, *conv, dtb[1], alog[1], rev=True,
                       extra=(y_f, dskip, prm["ssd_norm"].reshape(1, SSD_WIDTH)))

    hd = LRU_HEAD_DIM
    wg = jnp.concatenate([prm["lru_w_a"][0], prm["lru_w_x"][0],
                          prm["lru_w_a"][1], prm["lru_w_x"][1]], axis=-1).astype(BF16)
    bg = jnp.stack([prm["lru_b_a"][0].reshape(LRU_HEADS, hd), prm["lru_b_x"][0].reshape(LRU_HEADS, hd),
                    prm["lru_b_a"][1].reshape(LRU_HEADS, hd), prm["lru_b_x"][1].reshape(LRU_HEADS, hd)],
                   axis=1).reshape(LRU_HEADS, 1, 4 * hd)
    lcw = prm["lru_conv_w"]
    lcb = prm["lru_conv_b"].reshape(1, -1)
    lam = prm["lru_lambda"]
    cw_ = 8
    cr_ = clen // cw_
    xr_ctx = p_ctx[:, :, COL_XR:COL_XR + LRU_WIDTH]
    xr_ctx = xr_ctx.reshape(bsz, cw_, cr_, LRU_WIDTH).transpose(0, 2, 1, 3).reshape(bsz, clen, LRU_WIDTH)
    zero_h = jnp.zeros((bsz, 2, LRU_WIDTH), F32)
    (f_ctx,) = _lru(xr_ctx, 0, None, 0, lcw, lcb, wg, bg, lam, zero_h, rows=cr_, width=cw_, emit=False)
    rg, _ = _lru(p_lat, COL_XR, p_lat, COL_YR, lcw, lcb, wg, bg, lam, f_ctx,
                 rows=rows, width=GRID_W, emit=True)

    m = _merge_a(yn, rg, prm["w_out_ssd"].astype(BF16), prm["w_out_lru"].astype(BF16), p_lat, *TILES["merge_a"])
    x1 = _merge_b(m, prm["w_o"].astype(BF16), x, g_m, *TILES["merge_b"])

    act = _ffn_up(x1, sh_f, sc_f, prm["norm_ffn"].reshape(1, d), prm["ffn_w13"].astype(BF16), *TILES["ffn_up"])
    return _ffn_down(act, prm["ffn_w2"].astype(BF16), x1, g_f, prm["final_norm"].reshape(1, d), *TILES["ffn_down"])


def kernel(x, c, ctx, c_ctx, w_ada, b_ada, norm_mix, norm_ffn, w_in, ssd_conv_w, ssd_conv_b, ssd_dt_bias,
           ssd_a_log, ssd_d, ssd_norm, w_out_ssd, lru_conv_w, lru_conv_b, lru_w_a, lru_b_a, lru_w_x, lru_b_x,
           lru_lambda, w_out_lru, w_gate, b_gate, w_o, ffn_w13, ffn_w2, final_norm):
    assert w_ada.shape[0] == 1, "single-layer trunk"
    bsz = x.shape[0]
    cvec = jnp.zeros((8, D_MODEL), F32).at[:bsz].set(c).at[bsz].set(c_ctx)
    mod_all = _ada(cvec, w_ada[0], b_ada[0].reshape(1, -1))
    prm = dict(w_in=w_in[0], ssd_conv_w=ssd_conv_w[0], ssd_conv_b=ssd_conv_b[0], ssd_dt_bias=ssd_dt_bias[0],
               ssd_a_log=ssd_a_log[0], ssd_d=ssd_d[0], ssd_norm=ssd_norm[0], w_out_ssd=w_out_ssd[0],
               lru_conv_w=lru_conv_w[0], lru_conv_b=lru_conv_b[0], lru_w_a=lru_w_a[0], lru_b_a=lru_b_a[0],
               lru_w_x=lru_w_x[0], lru_b_x=lru_b_x[0], lru_lambda=lru_lambda[0], w_out_lru=w_out_lru[0],
               w_gate=w_gate[0], b_gate=b_gate[0], w_o=w_o[0], norm_mix=norm_mix[0], norm_ffn=norm_ffn[0],
               ffn_w13=ffn_w13[0], ffn_w2=ffn_w2[0], final_norm=final_norm)
    return _layer(x, ctx, mod_all[:bsz], mod_all[bsz], prm)
```

```python
import functools

import jax
import jax.numpy as jnp
from jax import lax
from jax.experimental import pallas as pl
from jax.experimental.pallas import tpu as pltpu

F32 = jnp.float32
BF16 = jnp.bfloat16

D_MODEL = 2048
GRID_W = 64
N_MOD = 6
SSD_HEAD_DIM = 64
SSD_HEADS = 32
SSD_WIDTH = SSD_HEADS * SSD_HEAD_DIM
SSD_GROUPS = 4
SSD_HPG = SSD_HEADS // SSD_GROUPS
SSD_STATE = 128
SSD_BC = SSD_GROUPS * SSD_STATE
SSD_CHUNK = 128
CONV_W = 4
CONV_PAD_LEFT = 2
LRU_WIDTH = D_MODEL
LRU_HEADS = 16
LRU_HEAD_DIM = LRU_WIDTH // LRU_HEADS
LRU_C = 8.0
EPS = 1e-6

LANES = 128
HALO = 16
VMEM_LIMIT = 56 * 1024 * 1024

COL_XS = 0
COL_XR = 2048
COL_Z = 4096
COL_YR = 6144
COL_GS = 8192
COL_GR = 10240
COL_BC = 12288
NP = 13312
CTX_COL_BC = 4096
CTX_NP = 5120
DT_COLS = 256


TILES = {"in_proj": (1024,), "merge_a": (512,), "merge_b": (512,), "ffn_up": (1024, 512), "ffn_down": (256,)}


def _params(sem, vmem=VMEM_LIMIT):
    return pltpu.CompilerParams(dimension_semantics=sem, vmem_limit_bytes=vmem)


LOG2E = 1.4426950408889634


def _sigmoid(x):
    return 1.0 / (1.0 + jnp.exp2(x * (-LOG2E)))


def _softplus(x):
    return jnp.maximum(x, 0.0) + jnp.log1p(jnp.exp(-jnp.abs(x)))


def _norm_mod(x, g, shift, scale):
    xf = x.astype(F32)
    ms = jnp.mean(xf * xf, axis=-1, keepdims=True)
    hn = xf * lax.rsqrt(ms + EPS) * g
    return hn * (1.0 + scale) + shift


def _ada_kernel(c_ref, w_ref, b_ref, o_ref):
    c = c_ref[...]
    s = (c * _sigmoid(c)).astype(BF16)
    o_ref[...] = jnp.dot(s, w_ref[...].astype(BF16), preferred_element_type=F32) + b_ref[...]


def _ada(cvec, w, b):
    m, d = cvec.shape
    n = w.shape[1]
    tn = 1024
    return pl.pallas_call(
        _ada_kernel,
        out_shape=jax.ShapeDtypeStruct((m, n), F32),
        grid=(n // tn,),
        in_specs=[pl.BlockSpec((m, d), lambda j: (0, 0)),
                  pl.BlockSpec((d, tn), lambda j: (0, j)),
                  pl.BlockSpec((1, tn), lambda j: (0, j))],
        out_specs=pl.BlockSpec((m, tn), lambda j: (0, j)),
        compiler_params=_params(("arbitrary",)),
        name="ada",
    )(cvec, w, b)


def _in_proj_kernel(x_ref, sh_ref, sc_ref, g_ref, w_ref, b_ref, wdt_ref, p_ref, dt_ref, hn_ref,
                    *, gate_lo, gate_hi):
    n = pl.program_id(2)

    @pl.when(n == 0)
    def _():
        h = _norm_mod(x_ref[...], g_ref[...], sh_ref[...], sc_ref[...]).astype(BF16)
        hn_ref[...] = h
        dt_ref[...] = jnp.dot(h, wdt_ref[...], preferred_element_type=F32)

    acc = jnp.dot(hn_ref[...], w_ref[...], preferred_element_type=F32) + b_ref[...]
    if gate_lo == gate_hi:
        p_ref[...] = acc.astype(BF16)
        return
    is_gate = jnp.logical_and(n >= gate_lo, n < gate_hi)

    @pl.when(is_gate)
    def _():
        p_ref[...] = _sigmoid(acc).astype(BF16)

    @pl.when(jnp.logical_not(is_gate))
    def _():
        p_ref[...] = acc.astype(BF16)


def _in_proj(x, shift, scale, g, w_cat, b_cat, w_dt, tm, ctx_subset):
    bsz, s, d = x.shape
    tn = 1024
    if ctx_subset:
        n_out = CTX_NP
        gates = (0, 0)
        wtile = lambda n: jnp.where(n < CTX_COL_BC // tn, n, COL_BC // tn)
    else:
        n_out = NP
        gates = (COL_GS // tn, COL_BC // tn)
        wtile = lambda n: n
    kern = functools.partial(_in_proj_kernel, gate_lo=gates[0], gate_hi=gates[1])
    return pl.pallas_call(
        kern,
        out_shape=(jax.ShapeDtypeStruct((bsz, s, n_out), BF16),
                   jax.ShapeDtypeStruct((bsz, s, DT_COLS), F32)),
        grid=(bsz, s // tm, n_out // tn),
        in_specs=[pl.BlockSpec((None, tm, d), lambda b, i, n: (b, i, 0)),
                  pl.BlockSpec((None, 1, d), lambda b, i, n: (b, 0, 0)),
                  pl.BlockSpec((None, 1, d), lambda b, i, n: (b, 0, 0)),
                  pl.BlockSpec((1, d), lambda b, i, n: (0, 0)),
                  pl.BlockSpec((d, tn), lambda b, i, n: (0, wtile(n))),
                  pl.BlockSpec((1, tn), lambda b, i, n: (0, wtile(n))),
                  pl.BlockSpec((d, DT_COLS), lambda b, i, n: (0, 0))],
        out_specs=(pl.BlockSpec((None, tm, tn), lambda b, i, n: (b, i, n)),
                   pl.BlockSpec((None, tm, DT_COLS), lambda b, i, n: (b, i, 0))),
        scratch_shapes=[pltpu.VMEM((tm, d), BF16)],
        compiler_params=_params(("arbitrary", "arbitrary", "arbitrary")),
        name="in_proj",
    )(x, shift, scale, g, w_cat, b_cat, w_dt)


def _split3(v):
    v1 = v.astype(BF16)
    r1 = v - v1.astype(F32)
    v2 = r1.astype(BF16)
    r2 = r1 - v2.astype(F32)
    return v1, v2, r2.astype(BF16)


def _ssd_kernel(*refs, rev, final, nc):
    if final:
        (xs_ref, xsp_ref, xsn_ref, bc_ref, bcp_ref, bcn_ref, dt_ref,
         cwx_ref, cbx_ref, cwbc_ref, cbbc_ref, dtb_ref, alog_ref, s0_ref,
         yf_ref, z_ref, dskip_ref, nw_ref,
         y_ref, sfin_ref, st_ref, gat_ref) = refs
    else:
        (xs_ref, xsp_ref, xsn_ref, bc_ref, bcp_ref, bcn_ref, dt_ref,
         cwx_ref, cbx_ref, cwbc_ref, cbbc_ref, dtb_ref, alog_ref, s0_ref,
         y_ref, sfin_ref, st_ref) = refs

    i = pl.program_id(1)
    c = (nc - 1 - i) if rev else i
    q = SSD_CHUNK

    @pl.when(i == 0)
    def _():
        st_ref[...] = s0_ref[...]

    has_prev = c > 0
    has_next = c < nc - 1

    tt = lax.broadcasted_iota(jnp.int32, (q, CONV_W * q), 0)
    rr = lax.broadcasted_iota(jnp.int32, (q, CONV_W * q), 1)
    shift_cur = jnp.where((rr % q) == tt + (rr // q) - CONV_PAD_LEFT, 1.0, 0.0).astype(BF16)
    t16 = lax.broadcasted_iota(jnp.int32, (HALO, 2 * HALO), 0)
    r16 = lax.broadcasted_iota(jnp.int32, (HALO, 2 * HALO), 1)
    top_sel = ((r16 == HALO - 2 + t16) & (t16 < 2)) | ((r16 == 2 * HALO - 1) & (t16 == 0))
    shift_top = jnp.where(top_sel, 1.0, 0.0).astype(BF16)
    tb16 = lax.broadcasted_iota(jnp.int32, (HALO, HALO), 0)
    rb16 = lax.broadcasted_iota(jnp.int32, (HALO, HALO), 1)
    shift_bot = jnp.where((tb16 == HALO - 1) & (rb16 == 0), 1.0, 0.0).astype(BF16)

    def conv_silu(cur_ref, prev_ref, next_ref, w_ref, b_ref):
        wb = [w_ref[k:k + 1, :].astype(BF16) for k in range(CONV_W)]
        cur = cur_ref[...]
        taps = jnp.concatenate([cur * wb[k] for k in range(CONV_W)], axis=0)
        out = jnp.dot(shift_cur, taps, preferred_element_type=F32) + b_ref[...]
        prev = jnp.where(has_prev, prev_ref[...], jnp.zeros_like(prev_ref))
        top = jnp.concatenate([prev * wb[0], prev * wb[1]], axis=0)
        nxt_rows = jnp.where(has_next, next_ref[...], jnp.zeros_like(next_ref))
        bot = nxt_rows * wb[CONV_W - 1]
        head = out[0:HALO] + jnp.dot(shift_top, top, preferred_element_type=F32)
        tail = out[q - HALO:q] + jnp.dot(shift_bot, bot, preferred_element_type=F32)
        out = jnp.concatenate([head, out[HALO:q - HALO], tail], axis=0)
        return out * _sigmoid(out)

    xs = conv_silu(xs_ref, xsp_ref, xsn_ref, cwx_ref, cbx_ref)
    bcv = conv_silu(bc_ref, bcp_ref, bcn_ref, cwbc_ref, cbbc_ref)

    dt = _softplus(dt_ref[...] + dtb_ref[...])
    a = dt * (-jnp.exp(alog_ref[...]))

    ii = lax.broadcasted_iota(jnp.int32, (q, q), 0)
    jj = lax.broadcasted_iota(jnp.int32, (q, q), 1)
    mask = (jj >= ii) if rev else (jj <= ii)
    tri = jnp.where(mask, 1.0, 0.0).astype(BF16)

    acum = sum(jnp.dot(tri, p, preferred_element_type=F32) for p in _split3(a))
    a_t = a.T
    dt_t = dt.T
    nt = (((1,), (1,)), ((), ()))
    acum_t = sum(lax.dot_general(p, tri, nt, preferred_element_type=F32) for p in _split3(a_t))

    edge = 0 if rev else q - 1
    total = acum[edge:edge + 1, :]
    cd = jnp.exp(total)
    w_t = dt_t * jnp.exp(acum_t[:, edge:edge + 1] - acum_t)
    acum2 = acum * LOG2E
    acum2_t = acum_t * LOG2E

    lane = lax.broadcasted_iota(jnp.int32, (q, LANES), 1)
    lo_half = lane < SSD_HEAD_DIM
    lo_row = lo_half[0:1, :]

    gsq = jnp.zeros((q, LANES), F32)
    for g in range(SSD_GROUPS):
        b_g = bcv[:, g * SSD_STATE:(g + 1) * SSD_STATE]
        c_g = bcv[:, SSD_BC + g * SSD_STATE:SSD_BC + (g + 1) * SSD_STATE]
        b_gb = b_g.astype(BF16)
        c_gb = c_g.astype(BF16)
        cb = lax.dot_general(c_gb, b_gb, nt, preferred_element_type=F32)
        gw = SSD_HPG * SSD_HEAD_DIM
        s_in = st_ref[:, g * gw:(g + 1) * gw].astype(BF16)
        yoff = jnp.dot(c_gb, s_in, preferred_element_type=F32)
        b_t = b_g.T
        for kp in range(SSD_HPG // 2):
            pair = g * (SSD_HPG // 2) + kp
            col = pair * LANES
            ms, ecols, bws, cds = [], [], [], []
            for e in (2 * pair, 2 * pair + 1):
                colb = jnp.broadcast_to(acum2[:, e:e + 1], (q, q))
                seg = colb - acum2_t[e:e + 1, :]
                lmat = jnp.exp2(jnp.where(mask, seg, -1e30))
                ms.append((cb * lmat * dt_t[e:e + 1, :]).astype(BF16))
                ecols.append(jnp.exp2(colb))
                bws.append((b_t * w_t[e:e + 1, :]).astype(BF16))
                cds.append(jnp.broadcast_to(cd[:, e:e + 1], (1, LANES)))
            xp = xs[:, col:col + LANES]
            rhs = jnp.concatenate([jnp.where(lo_half, xp, 0.0).astype(BF16),
                                   jnp.where(lo_half, 0.0, xp).astype(BF16)], axis=0)
            lhs = jnp.concatenate(ms, axis=1)
            ydiag = jnp.dot(lhs, rhs, preferred_element_type=F32)
            y_pair = ydiag + jnp.where(lo_half, ecols[0], ecols[1]) * yoff[:, kp * LANES:(kp + 1) * LANES]
            cd_pair = jnp.where(lo_row, cds[0], cds[1])
            ds = jnp.dot(jnp.concatenate(bws, axis=1), rhs, preferred_element_type=F32)
            st_ref[:, col:col + LANES] = st_ref[:, col:col + LANES] * cd_pair + ds
            if final:
                y_tot = y_pair + yf_ref[:, col:col + LANES] + dskip_ref[:, col:col + LANES] * xp
                zz = z_ref[:, col:col + LANES].astype(F32)
                gated = y_tot * (zz * _sigmoid(zz))
                gat_ref[:, col:col + LANES] = gated
                gsq = gsq + gated * gated
            else:
                y_ref[:, col:col + LANES] = y_pair

    if final:
        ms = jnp.sum(gsq, axis=-1, keepdims=True) * (1.0 / SSD_WIDTH)
        y_ref[...] = (gat_ref[...] * lax.rsqrt(ms + EPS) * nw_ref[...]).astype(BF16)

    @pl.when(i == nc - 1)
    def _():
        sfin_ref[...] = st_ref[...]


def _ssd_sweep(p, dtraw, s0, cwx, cbx, cwbc, cbbc, dtb, alog, *, rev, col_bc, extra=None):
    bsz, s, _ = p.shape
    nc = s // SSD_CHUNK
    q = SSD_CHUNK
    hb = q // HALO
    final = extra is not None

    def cidx(i):
        return (nc - 1 - i) if rev else i

    def cur(col_block):
        return lambda b, i: (b, cidx(i), col_block)

    def prev(col_block):
        return lambda b, i: (b, jnp.maximum(cidx(i) * hb - 1, 0), col_block)

    def nxt(col_block):
        return lambda b, i: (b, jnp.minimum(cidx(i) * hb + hb, s // HALO - 1), col_block)

    const2 = lambda b, i: (0, 0)
    xw_, bw_ = SSD_WIDTH, 2 * SSD_BC
    in_specs = [pl.BlockSpec((None, q, xw_), cur(COL_XS // xw_)),
                pl.BlockSpec((None, HALO, xw_), prev(COL_XS // xw_)),
                pl.BlockSpec((None, HALO, xw_), nxt(COL_XS // xw_)),
                pl.BlockSpec((None, q, bw_), cur(col_bc // bw_)),
                pl.BlockSpec((None, HALO, bw_), prev(col_bc // bw_)),
                pl.BlockSpec((None, HALO, bw_), nxt(col_bc // bw_)),
                pl.BlockSpec((None, q, LANES), cur(1 if rev else 0)),
                pl.BlockSpec((CONV_W, xw_), const2),
                pl.BlockSpec((1, xw_), const2),
                pl.BlockSpec((CONV_W, bw_), const2),
                pl.BlockSpec((1, bw_), const2),
                pl.BlockSpec((1, LANES), const2),
                pl.BlockSpec((1, LANES), const2),
                pl.BlockSpec((None, SSD_STATE, xw_), lambda b, i: (b, 0, 0))]
    args = [p, p, p, p, p, p, dtraw, cwx, cbx, cwbc, cbbc, dtb, alog, s0]
    scratch = [pltpu.VMEM((SSD_STATE, xw_), F32)]
    if final:
        yf, dskip, nw = extra
        in_specs += [pl.BlockSpec((None, q, xw_), cur(0)),
                     pl.BlockSpec((None, q, xw_), cur(COL_Z // xw_)),
                     pl.BlockSpec((1, xw_), const2),
                     pl.BlockSpec((1, xw_), const2)]
        args += [yf, p, dskip, nw]
        scratch.append(pltpu.VMEM((q, xw_), F32))
    y_dtype = BF16 if final else F32
    kern = functools.partial(_ssd_kernel, rev=rev, final=final, nc=nc)
    return pl.pallas_call(
        kern,
        out_shape=(jax.ShapeDtypeStruct((bsz, s, xw_), y_dtype),
                   jax.ShapeDtypeStruct((bsz, SSD_STATE, xw_), F32)),
        grid=(bsz, nc),
        in_specs=in_specs,
        out_specs=(pl.BlockSpec((None, q, xw_), cur(0)),
                   pl.BlockSpec((None, SSD_STATE, xw_), lambda b, i: (b, 0, 0))),
        scratch_shapes=scratch,
        compiler_params=_params(("arbitrary", "arbitrary")),
        name="ssd_" + ("bwd" if rev else "fwd") + ("_final" if final else ""),
    )(*args)


def _lru_kernel(*refs, rows, width, tb, emit):
    if emit:
        (xr_ref, yr_ref, cw_ref, cb_ref, wg_ref, bg_ref, lam_ref, h0_ref,
         rg_ref, fin_ref,
         xe_ref, af_ref, uf_ref, ab_ref, ub_ref, sh_ref, fl_ref, pe_ref, cinf_ref, cinb_ref) = refs
    else:
        (xr_ref, cw_ref, cb_ref, wg_ref, bg_ref, lam_ref, h0_ref,
         fin_ref,
         xe_ref, af_ref, uf_ref, ab_ref, ub_ref, sh_ref, fl_ref, pe_ref, cinf_ref, cinb_ref) = refs
    R, W = rows, width
    hd = LRU_HEAD_DIM
    rb = tb // W

    xe_ref[2:2 + R] = xr_ref[...].astype(F32).reshape(R, W, hd)
    zero8 = jnp.zeros((8, hd), F32)
    sh_ref[0:8, :] = zero8
    sh_ref[8 + W:16 + W, :] = zero8
    for src, dst, off in ((R - 2, 0, 7), (R - 1, 1, 7), (0, R + 2, 9)):
        sh_ref[8:8 + W, :] = xe_ref[2 + src]
        xe_ref[dst] = sh_ref[off:off + W, :]

    a_scale = _softplus(-lam_ref[...]) * (-LRU_C * LOG2E)
    cwv = cw_ref[...]
    cbv = cb_ref[...]
    wg = wg_ref[...]
    bg = bg_ref[...]

    def gate_body(blk, carry):
        r0 = blk * rb
        xc = cbv
        for k in range(CONV_W):
            xc = xc + cwv[k:k + 1, :] * xe_ref[pl.ds(r0 + k, rb)]
        xc2 = xc.reshape(rb * W, hd)
        pre = jnp.dot(xc2.astype(BF16), wg, preferred_element_type=F32) + bg
        for d, (a_ref, u_ref) in enumerate(((af_ref, uf_ref), (ab_ref, ub_ref))):
            r_gate = _sigmoid(pre[:, (2 * d) * hd:(2 * d + 1) * hd])
            i_gate = _sigmoid(pre[:, (2 * d + 1) * hd:(2 * d + 2) * hd])
            a = jnp.exp2(r_gate * a_scale[d:d + 1, :])
            v = (1.0 - a) * (1.0 + a)
            u = (v * lax.rsqrt(jnp.maximum(v, 1e-37))) * (i_gate * xc2)
            a_ref[pl.ds(r0, rb)] = a.reshape(rb, W, hd)
            u_ref[pl.ds(r0, rb)] = u.reshape(rb, W, hd)
        return carry

    lax.fori_loop(0, R // rb, gate_body, 0)

    zeros = jnp.zeros((W, hd), F32)
    ones = jnp.ones((W, hd), F32)

    def scan_step(a_ref, u_ref, r, h, p):
        a = a_ref[r]
        h = a * h + u_ref[r]
        p = a * p
        u_ref[r] = h
        a_ref[r] = p
        return h, p

    def scan_body(t, carry):
        hf, pf, hb, pb = carry
        hf, pf = scan_step(af_ref, uf_ref, t, hf, pf)
        hb, pb = scan_step(ab_ref, ub_ref, R - 1 - t, hb, pb)
        return hf, pf, hb, pb

    def carry_chain(h_end, p_end, h0, cin_ref, reverse):
        fl_ref[...] = h_end
        pe_ref[...] = p_end

        def body(t, carry):
            cc = (W - 1 - t) if reverse else t
            cin_ref[pl.ds(cc, 1), :] = carry
            return fl_ref[pl.ds(cc, 1), :] + pe_ref[pl.ds(cc, 1), :] * carry
        return lax.fori_loop(0, W, body, h0)

    hf_end, pf_end, hb_end, pb_end = lax.fori_loop(0, R, scan_body, (zeros, ones, zeros, ones), unroll=4)
    fin_f = carry_chain(hf_end, pf_end, h0_ref[0:1, :], cinf_ref, False)
    fin_b = carry_chain(hb_end, pb_end, h0_ref[1:2, :], cinb_ref, True)
    fin_ref[0:1, :] = fin_f
    fin_ref[1:2, :] = fin_b

    if emit:
        cin_f = cinf_ref[...]
        cin_b = cinb_ref[...]

        def out_body(r, carry):
            h = (uf_ref[r] + af_ref[r] * cin_f) + (ub_ref[r] + ab_ref[r] * cin_b)
            row0 = pl.multiple_of(r * W, W)
            y = yr_ref[pl.ds(row0, W), :].astype(F32)
            rg_ref[pl.ds(row0, W), :] = (h * jax.nn.gelu(y)).astype(BF16)
            return carry

        lax.fori_loop(0, R, out_body, 0)


def _lru(xr_src, xr_col, yr_src, yr_col, cw, cb, wg, bg, lam, h0, *, rows, width, emit):
    bsz, s, _ = xr_src.shape
    hd = LRU_HEAD_DIM
    tb = min(512, s)
    kern = functools.partial(_lru_kernel, rows=rows, width=width, tb=tb, emit=emit)
    head = lambda b, h: (0, h)
    in_specs = [pl.BlockSpec((None, s, hd), lambda b, h: (b, 0, xr_col // hd + h))]
    args = [xr_src]
    if emit:
        in_specs.append(pl.BlockSpec((None, s, hd), lambda b, h: (b, 0, yr_col // hd + h)))
        args.append(yr_src)
    in_specs += [pl.BlockSpec((CONV_W, hd), head),
                 pl.BlockSpec((1, hd), head),
                 pl.BlockSpec((None, hd, 4 * hd), lambda b, h: (h, 0, 0)),
                 pl.BlockSpec((None, 1, 4 * hd), lambda b, h: (h, 0, 0)),
                 pl.BlockSpec((2, hd), head),
                 pl.BlockSpec((None, 2, hd), lambda b, h: (b, 0, h))]
    args += [cw, cb, wg, bg, lam, h0]
    out_shape = [jax.ShapeDtypeStruct((bsz, 2, LRU_WIDTH), F32)]
    out_specs = [pl.BlockSpec((None, 2, hd), lambda b, h: (b, 0, h))]
    if emit:
        out_shape.insert(0, jax.ShapeDtypeStruct((bsz, s, LRU_WIDTH), BF16))
        out_specs.insert(0, pl.BlockSpec((None, s, hd), lambda b, h: (b, 0, h)))
    big = pltpu.VMEM((rows, width, hd), F32)
    scratch = [pltpu.VMEM((rows + 3, width, hd), F32), big, big, big, big,
               pltpu.VMEM((width + 16, hd), F32),
               pltpu.VMEM((width, hd), F32), pltpu.VMEM((width, hd), F32),
               pltpu.VMEM((width, hd), F32), pltpu.VMEM((width, hd), F32)]
    return pl.pallas_call(
        kern,
        out_shape=tuple(out_shape),
        grid=(bsz, LRU_HEADS),
        in_specs=in_specs,
        out_specs=tuple(out_specs),
        scratch_shapes=scratch,
        compiler_params=_params(("arbitrary", "arbitrary")),
        name="lru" + ("" if emit else "_ctx"),
    )(*args)


def _merge_a_kernel(yn_ref, rg_ref, ws_ref, wl_ref, gs_ref, gr_ref, m_ref):
    o_s = jnp.dot(yn_ref[...], ws_ref[...], preferred_element_type=F32)
    o_r = jnp.dot(rg_ref[...], wl_ref[...], preferred_element_type=F32)
    m_ref[...] = (gs_ref[...].astype(F32) * o_s + gr_ref[...].astype(F32) * o_r).astype(BF16)


def _merge_a(yn, rg, ws, wl, p, tm):
    bsz, s, d = yn.shape
    row = lambda b, i: (b, i, 0)
    whole = lambda b, i: (0, 0)
    return pl.pallas_call(
        _merge_a_kernel,
        out_shape=jax.ShapeDtypeStruct((bsz, s, d), BF16),
        grid=(bsz, s // tm),
        in_specs=[pl.BlockSpec((None, tm, d), row),
                  pl.BlockSpec((None, tm, d), row),
                  pl.BlockSpec((d, d), whole),
                  pl.BlockSpec((d, d), whole),
                  pl.BlockSpec((None, tm, d), lambda b, i: (b, i, COL_GS // d)),
                  pl.BlockSpec((None, tm, d), lambda b, i: (b, i, COL_GR // d))],
        out_specs=pl.BlockSpec((None, tm, d), row),
        compiler_params=_params(("arbitrary", "arbitrary")),
        name="merge_a",
    )(yn, rg, ws, wl, p, p)


def _merge_b_kernel(m_ref, w_ref, x_ref, gm_ref, sh_ref, sc_ref, g_ref, x1_ref, h_ref):
    mix = jnp.dot(m_ref[...], w_ref[...], preferred_element_type=F32)
    x1 = x_ref[...] + gm_ref[...] * mix
    x1_ref[...] = x1
    h_ref[...] = _norm_mod(x1, g_ref[...], sh_ref[...], sc_ref[...]).astype(BF16)


def _merge_b(m, w_o, x, gm, shift, scale, g, tm):
    bsz, s, d = m.shape
    row = lambda b, i: (b, i, 0)
    vec = lambda b, i: (b, 0, 0)
    return pl.pallas_call(
        _merge_b_kernel,
        out_shape=(jax.ShapeDtypeStruct((bsz, s, d), F32), jax.ShapeDtypeStruct((bsz, s, d), BF16)),
        grid=(bsz, s // tm),
        in_specs=[pl.BlockSpec((None, tm, d), row),
                  pl.BlockSpec((d, d), lambda b, i: (0, 0)),
                  pl.BlockSpec((None, tm, d), row),
                  pl.BlockSpec((None, 1, d), vec),
                  pl.BlockSpec((None, 1, d), vec),
                  pl.BlockSpec((None, 1, d), vec),
                  pl.BlockSpec((1, d), lambda b, i: (0, 0))],
        out_specs=(pl.BlockSpec((None, tm, d), row), pl.BlockSpec((None, tm, d), row)),
        compiler_params=_params(("arbitrary", "arbitrary")),
        name="merge_b",
    )(m, w_o, x, gm, shift, scale, g)


def _ffn_up_kernel(h_ref, w1_ref, w3_ref, a_ref):
    h = h_ref[...]
    gate = jnp.dot(h, w1_ref[...], preferred_element_type=F32)
    up = jnp.dot(h, w3_ref[...], preferred_element_type=F32)
    a_ref[...] = (gate * _sigmoid(gate) * up).astype(BF16)


def _ffn_up(h, w13, tm, tn):
    bsz, s, d = h.shape
    hid = w13.shape[1] // 2
    nb = hid // tn
    return pl.pallas_call(
        _ffn_up_kernel,
        out_shape=jax.ShapeDtypeStruct((bsz, s, hid), BF16),
        grid=(bsz, s // tm, nb),
        in_specs=[pl.BlockSpec((None, tm, d), lambda b, i, n: (b, i, 0)),
                  pl.BlockSpec((d, tn), lambda b, i, n: (0, n)),
                  pl.BlockSpec((d, tn), lambda b, i, n: (0, nb + n))],
        out_specs=pl.BlockSpec((None, tm, tn), lambda b, i, n: (b, i, n)),
        compiler_params=_params(("arbitrary", "arbitrary", "arbitrary")),
        name="ffn_up",
    )(h, w13, w13)


def _ffn_down_kernel(a_ref, w_ref, x_ref, gf_ref, fn_ref, o_ref):
    x2 = x_ref[...] + gf_ref[...] * jnp.dot(a_ref[...], w_ref[...], preferred_element_type=F32)
    ms = jnp.mean(x2 * x2, axis=-1, keepdims=True)
    o_ref[...] = x2 * lax.rsqrt(ms + EPS) * fn_ref[...]


def _ffn_down(a, w2, x1, gf, fnorm, tm):
    bsz, s, hid = a.shape
    d = w2.shape[1]
    row = lambda b, i: (b, i, 0)
    return pl.pallas_call(
        _ffn_down_kernel,
        out_shape=jax.ShapeDtypeStruct((bsz, s, d), F32),
        grid=(bsz, s // tm),
        in_specs=[pl.BlockSpec((None, tm, hid), row),
                  pl.BlockSpec((hid, d), lambda b, i: (0, 0)),
                  pl.BlockSpec((None, tm, d), row),
                  pl.BlockSpec((None, 1, d), lambda b, i: (b, 0, 0)),
                  pl.BlockSpec((1, d), lambda b, i: (0, 0))],
        out_specs=pl.BlockSpec((None, tm, d), row),
        compiler_params=_params(("arbitrary", "arbitrary")),
        name="ffn_down",
    )(a, w2, x1, gf, fnorm)


def _pad_lanes(v, n=LANES):
    return jnp.pad(v, (0, n - v.shape[0])).reshape(1, n)


def _layer(x, ctx, mod, mod_c, prm):
    bsz, seq, d = x.shape
    clen = ctx.shape[1]
    rows = seq // GRID_W

    sh_m, sc_m, g_m, sh_f, sc_f, g_f = [m.reshape(bsz, 1, d) for m in jnp.split(mod, N_MOD, axis=-1)]
    csh_m, csc_m = [jnp.broadcast_to(m.reshape(1, 1, d), (bsz, 1, d))
                    for m in jnp.split(mod_c, N_MOD, axis=-1)[:2]]

    w_in = prm["w_in"]
    o = 0
    parts = {}
    for name, size in (("z", SSD_WIDTH), ("xs", SSD_WIDTH), ("b", SSD_BC), ("c", SSD_BC),
                       ("dt", 2 * SSD_HEADS), ("xr", LRU_WIDTH), ("yr", LRU_WIDTH)):
        parts[name] = w_in[:, o:o + size]
        o += size
    w_cat = jnp.concatenate([parts[k].astype(BF16) for k in ("xs", "xr", "z", "yr")]
                            + [prm["w_gate"].astype(BF16), parts["b"].astype(BF16), parts["c"].astype(BF16)], axis=1)
    b_cat = jnp.concatenate([jnp.zeros((COL_GS,), F32), prm["b_gate"],
                             jnp.zeros((NP - COL_BC,), F32)]).reshape(1, NP)
    w_dt = jnp.zeros((d, DT_COLS), F32)
    w_dt = w_dt.at[:, 0:SSD_HEADS].set(parts["dt"][:, :SSD_HEADS])
    w_dt = w_dt.at[:, LANES:LANES + SSD_HEADS].set(parts["dt"][:, SSD_HEADS:]).astype(BF16)
    g_mix = prm["norm_mix"].reshape(1, d)

    p_ctx, dt_ctx = _in_proj(ctx, csh_m, csc_m, g_mix, w_cat, b_cat, w_dt, tm=clen, ctx_subset=True)
    p_lat, dt_lat = _in_proj(x, sh_m, sc_m, g_mix, w_cat, b_cat, w_dt, *TILES["in_proj"], ctx_subset=False)

    cwx = prm["ssd_conv_w"][:, :SSD_WIDTH]
    cwbc = prm["ssd_conv_w"][:, SSD_WIDTH:]
    cbx = prm["ssd_conv_b"][:SSD_WIDTH].reshape(1, -1)
    cbbc = prm["ssd_conv_b"][SSD_WIDTH:].reshape(1, -1)
    dtb = [_pad_lanes(prm["ssd_dt_bias"][k]) for k in range(2)]
    alog = [_pad_lanes(prm["ssd_a_log"][k]) for k in range(2)]
    conv = (cwx, cbx, cwbc, cbbc)
    zero_state = jnp.zeros((bsz, SSD_STATE, SSD_WIDTH), F32)
    _, s_f = _ssd_sweep(p_ctx, dt_ctx, zero_state, *conv, dtb[0], alog[0], rev=False, col_bc=CTX_COL_BC)
    _, s_b = _ssd_sweep(p_ctx, dt_ctx, zero_state, *conv, dtb[1], alog[1], rev=True, col_bc=CTX_COL_BC)
    y_f, _ = _ssd_sweep(p_lat, dt_lat, s_f, *conv, dtb[0], alog[0], rev=False, col_bc=COL_BC)
    dskip = jnp.repeat(prm["ssd_d"], SSD_HEAD_DIM).reshape(1, SSD_WIDTH)
    yn, _ = _ssd_sweep(p_lat, dt_lat, s_b, *conv, dtb[1], alog[1], rev=True, col_bc=COL_BC,
                       extra=(y_f, dskip, prm["ssd_norm"].reshape(1, SSD_WIDTH)))

    hd = LRU_HEAD_DIM
    wg = jnp.concatenate([prm["lru_w_a"][0], prm["lru_w_x"][0],
                          prm["lru_w_a"][1], prm["lru_w_x"][1]], axis=-1).astype(BF16)
    bg = jnp.stack([prm["lru_b_a"][0].reshape(LRU_HEADS, hd), prm["lru_b_x"][0].reshape(LRU_HEADS, hd),
                    prm["lru_b_a"][1].reshape(LRU_HEADS, hd), prm["lru_b_x"][1].reshape(LRU_HEADS, hd)],
                   axis=1).reshape(LRU_HEADS, 1, 4 * hd)
    lcw = prm["lru_conv_w"]
    lcb = prm["lru_conv_b"].reshape(1, -1)
    lam = prm["lru_lambda"]
    cw_ = 8
    cr_ = clen // cw_
    xr_ctx = p_ctx[:, :, COL_XR:COL_XR + LRU_WIDTH]
    xr_ctx = xr_ctx.reshape(bsz, cw_, cr_, LRU_WIDTH).transpose(0, 2, 1, 3).reshape(bsz, clen, LRU_WIDTH)
    zero_h = jnp.zeros((bsz, 2, LRU_WIDTH), F32)
    (f_ctx,) = _lru(xr_ctx, 0, None, 0, lcw, lcb, wg, bg, lam, zero_h, rows=cr_, width=cw_, emit=False)
    rg, _ = _lru(p_lat, COL_XR, p_lat, COL_YR, lcw, lcb, wg, bg, lam, f_ctx,
                 rows=rows, width=GRID_W, emit=True)

    m = _merge_a(yn, rg, prm["w_out_ssd"].astype(BF16), prm["w_out_lru"].astype(BF16), p_lat, *TILES["merge_a"])
    x1, h_ffn = _merge_b(m, prm["w_o"].astype(BF16), x, g_m, sh_f, sc_f, prm["norm_ffn"].reshape(1, d),
                         *TILES["merge_b"])

    act = _ffn_up(h_ffn, prm["ffn_w13"].astype(BF16), *TILES["ffn_up"])
    return _ffn_down(act, prm["ffn_w2"].astype(BF16), x1, g_f, prm["final_norm"].reshape(1, d), *TILES["ffn_down"])


def kernel(x, c, ctx, c_ctx, w_ada, b_ada, norm_mix, norm_ffn, w_in, ssd_conv_w, ssd_conv_b, ssd_dt_bias,
           ssd_a_log, ssd_d, ssd_norm, w_out_ssd, lru_conv_w, lru_conv_b, lru_w_a, lru_b_a, lru_w_x, lru_b_x,
           lru_lambda, w_out_lru, w_gate, b_gate, w_o, ffn_w13, ffn_w2, final_norm):
    assert w_ada.shape[0] == 1, "single-layer trunk"
    bsz = x.shape[0]
    cvec = jnp.zeros((8, D_MODEL), F32).at[:bsz].set(c).at[bsz].set(c_ctx)
    mod_all = _ada(cvec, w_ada[0], b_ada[0].reshape(1, -1))
    prm = dict(w_in=w_in[0], ssd_conv_w=ssd_conv_w[0], ssd_conv_b=ssd_conv_b[0], ssd_dt_bias=ssd_dt_bias[0],
               ssd_a_log=ssd_a_log[0], ssd_d=ssd_d[0], ssd_norm=ssd_norm[0], w_out_ssd=w_out_ssd[0],
               lru_conv_w=lru_conv_w[0], lru_conv_b=lru_conv_b[0], lru_w_a=lru_w_a[0], lru_b_a=lru_b_a[0],
               lru_w_x=lru_w_x[0], lru_b_x=lru_b_x[0], lru_lambda=lru_lambda[0], w_out_lru=w_out_lru[0],
               w_gate=w_gate[0], b_gate=b_gate[0], w_o=w_o[0], norm_mix=norm_mix[0], norm_ffn=norm_ffn[0],
               ffn_w13=ffn_w13[0], ffn_w2=ffn_w2[0], final_norm=final_norm)
    return _layer(x, ctx, mod_all[:bsz], mod_all[bsz], prm)
```

```python
import functools

import jax
import numpy as np
import jax.numpy as jnp
from jax import lax
from jax.experimental import pallas as pl
from jax.experimental.pallas import tpu as pltpu

F32 = jnp.float32
BF16 = jnp.bfloat16

D_MODEL = 2048
GRID_W = 64
N_MOD = 6
SSD_HEAD_DIM = 64
SSD_HEADS = 32
SSD_WIDTH = SSD_HEADS * SSD_HEAD_DIM
SSD_GROUPS = 4
SSD_HPG = SSD_HEADS // SSD_GROUPS
SSD_STATE = 128
SSD_BC = SSD_GROUPS * SSD_STATE
SSD_CHUNK = 128
CONV_W = 4
CONV_PAD_LEFT = 2
LRU_WIDTH = D_MODEL
LRU_HEADS = 16
LRU_HEAD_DIM = LRU_WIDTH // LRU_HEADS
LRU_C = 8.0
EPS = 1e-6

LANES = 128
HALO = 16
VMEM_LIMIT = 56 * 1024 * 1024

COL_XS = 0
COL_XR = 2048
COL_Z = 4096
COL_YR = 6144
COL_GS = 8192
COL_GR = 10240
COL_BC = 12288
NP = 13312
CTX_COL_BC = 4096
CTX_NP = 5120
DT_COLS = 256


TILES = {"in_proj": (1024,), "merge_a": (512,), "merge_b": (512,), "ffn_up": (1024, 512), "ffn_down": (256,)}


def _params(sem, vmem=VMEM_LIMIT):
    return pltpu.CompilerParams(dimension_semantics=sem, vmem_limit_bytes=vmem)


LOG2E = 1.4426950408889634


def _sigmoid(x):
    return 1.0 / (1.0 + jnp.exp2(x * (-LOG2E)))


def _softplus(x):
    return jnp.maximum(x, 0.0) + jnp.log1p(jnp.exp(-jnp.abs(x)))


def _gelu_tanh(x):
    c1 = 0.7978845608028654
    half_x = 0.5 * x
    return half_x * jnp.tanh(x * (c1 + (c1 * 0.044715) * (x * x))) + half_x


def _norm_mod(x, g, shift, scale):
    xf = x.astype(F32)
    ms = jnp.mean(xf * xf, axis=-1, keepdims=True)
    hn = xf * lax.rsqrt(ms + EPS) * g
    return hn * (1.0 + scale) + shift


def _ada_kernel(c_ref, w_ref, b_ref, o_ref):
    c = c_ref[...]
    s = (c * _sigmoid(c)).astype(BF16)
    o_ref[...] = jnp.dot(s, w_ref[...].astype(BF16), preferred_element_type=F32) + b_ref[...]


def _ada(cvec, w, b):
    m, d = cvec.shape
    n = w.shape[1]
    tn = 1024
    return pl.pallas_call(
        _ada_kernel,
        out_shape=jax.ShapeDtypeStruct((m, n), F32),
        grid=(n // tn,),
        in_specs=[pl.BlockSpec((m, d), lambda j: (0, 0)),
                  pl.BlockSpec((d, tn), lambda j: (0, j)),
                  pl.BlockSpec((1, tn), lambda j: (0, j))],
        out_specs=pl.BlockSpec((m, tn), lambda j: (0, j)),
        compiler_params=_params(("arbitrary",)),
        name="ada",
    )(cvec, w, b)


def _in_proj_kernel(x_ref, sh_ref, sc_ref, g_ref, w_ref, b_ref, wdt_ref, p_ref, dt_ref, hn_ref,
                    *, gate_lo, gate_hi):
    n = pl.program_id(2)

    @pl.when(n == 0)
    def _():
        h = _norm_mod(x_ref[...], g_ref[...], sh_ref[...], sc_ref[...]).astype(BF16)
        hn_ref[...] = h
        dt_ref[...] = jnp.dot(h, wdt_ref[...], preferred_element_type=F32)

    acc = jnp.dot(hn_ref[...], w_ref[...], preferred_element_type=F32) + b_ref[...]
    if gate_lo == gate_hi:
        p_ref[...] = acc.astype(BF16)
        return
    is_gate = jnp.logical_and(n >= gate_lo, n < gate_hi)
    p_ref[...] = jnp.where(is_gate, _sigmoid(acc), acc).astype(BF16)


def _in_proj(x, shift, scale, g, w_cat, b_cat, w_dt, tm, ctx_subset):
    bsz, s, d = x.shape
    tn = 1024
    if ctx_subset:
        n_out = CTX_NP
        gates = (0, 0)
        wtile = lambda n: jnp.where(n < CTX_COL_BC // tn, n, COL_BC // tn)
    else:
        n_out = NP
        gates = (COL_GS // tn, COL_BC // tn)
        wtile = lambda n: n
    kern = functools.partial(_in_proj_kernel, gate_lo=gates[0], gate_hi=gates[1])
    return pl.pallas_call(
        kern,
        out_shape=(jax.ShapeDtypeStruct((bsz, s, n_out), BF16),
                   jax.ShapeDtypeStruct((bsz, s, DT_COLS), F32)),
        grid=(bsz, s // tm, n_out // tn),
        in_specs=[pl.BlockSpec((None, tm, d), lambda b, i, n: (b, i, 0)),
                  pl.BlockSpec((None, 1, d), lambda b, i, n: (b, 0, 0)),
                  pl.BlockSpec((None, 1, d), lambda b, i, n: (b, 0, 0)),
                  pl.BlockSpec((1, d), lambda b, i, n: (0, 0)),
                  pl.BlockSpec((d, tn), lambda b, i, n: (0, wtile(n))),
                  pl.BlockSpec((1, tn), lambda b, i, n: (0, wtile(n))),
                  pl.BlockSpec((d, DT_COLS), lambda b, i, n: (0, 0))],
        out_specs=(pl.BlockSpec((None, tm, tn), lambda b, i, n: (b, i, n)),
                   pl.BlockSpec((None, tm, DT_COLS), lambda b, i, n: (b, i, 0))),
        scratch_shapes=[pltpu.VMEM((tm, d), BF16)],
        compiler_params=_params(("arbitrary", "arbitrary", "arbitrary")),
        name="in_proj",
    )(x, shift, scale, g, w_cat, b_cat, w_dt)


def _split3(v):
    v1 = v.astype(BF16)
    r1 = v - v1.astype(F32)
    v2 = r1.astype(BF16)
    r2 = r1 - v2.astype(F32)
    return v1, v2, r2.astype(BF16)


def _ssd_kernel(*refs, rev, final, nc):
    it = iter(refs)
    if rev:
        xsc_ref, bcc_ref = next(it), next(it)
    else:
        xs_ref, xsp_ref, xsn_ref, bc_ref, bcp_ref, bcn_ref = [next(it) for _ in range(6)]
        cwx_ref, cbx_ref, cwbc_ref, cbbc_ref, shc_ref, sht_ref, shb_ref = [next(it) for _ in range(7)]
    dt_ref, dtb_ref, alog_ref, tri_ref, s0_ref = [next(it) for _ in range(5)]
    if final:
        yf_ref, z_ref, dskip_ref, nw_ref = [next(it) for _ in range(4)]
    y_ref, sfin_ref = next(it), next(it)
    if not rev:
        xsc_out_ref, bcc_out_ref = next(it), next(it)
    st_ref = next(it)
    if final:
        gat_ref = next(it)

    i = pl.program_id(1)
    c = (nc - 1 - i) if rev else i
    q = SSD_CHUNK

    @pl.when(i == 0)
    def _():
        st_ref[...] = s0_ref[...]

    tri = tri_ref[...]

    if rev:
        xs = xsc_ref[...].astype(F32)
        bcv = bcc_ref[...].astype(F32)
    else:
        has_prev = c > 0
        has_next = c < nc - 1
        shift_cur = shc_ref[...]
        shift_top = sht_ref[...]
        shift_bot = shb_ref[...]

        def conv_silu(cur_ref, prev_ref, next_ref, w_ref, b_ref):
            wb = [w_ref[k:k + 1, :].astype(BF16) for k in range(CONV_W)]
            cur = cur_ref[...]
            taps = jnp.concatenate([cur * wb[k] for k in range(CONV_W)], axis=0)
            out = jnp.dot(shift_cur, taps, preferred_element_type=F32) + b_ref[...]
            prev = jnp.where(has_prev, prev_ref[...], jnp.zeros_like(prev_ref))
            top = jnp.concatenate([prev * wb[0], prev * wb[1]], axis=0)
            nxt_rows = jnp.where(has_next, next_ref[...], jnp.zeros_like(next_ref))
            bot = nxt_rows * wb[CONV_W - 1]
            head = out[0:HALO] + jnp.dot(shift_top, top, preferred_element_type=F32)
            tail = out[q - HALO:q] + jnp.dot(shift_bot, bot, preferred_element_type=F32)
            out = jnp.concatenate([head, out[HALO:q - HALO], tail], axis=0)
            return out * _sigmoid(out)

        xs = conv_silu(xs_ref, xsp_ref, xsn_ref, cwx_ref, cbx_ref)
        bcv = conv_silu(bc_ref, bcp_ref, bcn_ref, cwbc_ref, cbbc_ref)
        xsc_out_ref[...] = xs.astype(BF16)
        bcc_out_ref[...] = bcv.astype(BF16)

    dt = _softplus(dt_ref[...] + dtb_ref[...])
    a = dt * (-jnp.exp(alog_ref[...]))

    ii = lax.broadcasted_iota(jnp.int32, (q, q), 0)
    jj = lax.broadcasted_iota(jnp.int32, (q, q), 1)
    mask = (jj >= ii) if rev else (jj <= ii)

    acum = sum(jnp.dot(tri, p, preferred_element_type=F32) for p in _split3(a))
    a_t = a.T
    dt_t = dt.T
    nt = (((1,), (1,)), ((), ()))
    acum_t = sum(lax.dot_general(p, tri, nt, preferred_element_type=F32) for p in _split3(a_t))

    edge = 0 if rev else q - 1
    total = acum[edge:edge + 1, :]
    cd = jnp.exp(total)
    w_t = dt_t * jnp.exp(acum_t[:, edge:edge + 1] - acum_t)
    acum2 = acum * LOG2E
    adj2_t = acum_t * LOG2E - jnp.log2(dt_t)

    lane = lax.broadcasted_iota(jnp.int32, (q, LANES), 1)
    lo_half = lane < SSD_HEAD_DIM
    lo_row = lo_half[0:1, :]

    gsq = jnp.zeros((q, LANES), F32)
    for g in range(SSD_GROUPS):
        b_g = bcv[:, g * SSD_STATE:(g + 1) * SSD_STATE]
        c_g = bcv[:, SSD_BC + g * SSD_STATE:SSD_BC + (g + 1) * SSD_STATE]
        b_gb = b_g.astype(BF16)
        c_gb = c_g.astype(BF16)
        cb = lax.dot_general(c_gb, b_gb, nt, preferred_element_type=F32)
        gw = SSD_HPG * SSD_HEAD_DIM
        s_in = st_ref[:, g * gw:(g + 1) * gw].astype(BF16)
        yoff = jnp.dot(c_gb, s_in, preferred_element_type=F32)
        b_t = b_g.T
        for kp in range(SSD_HPG // 2):
            pair = g * (SSD_HPG // 2) + kp
            col = pair * LANES
            ms, ecols, bws, cds = [], [], [], []
            for e in (2 * pair, 2 * pair + 1):
                colb = jnp.broadcast_to(acum2[:, e:e + 1], (q, q))
                seg = colb - adj2_t[e:e + 1, :]
                lmat_dt = jnp.exp2(jnp.where(mask, seg, -1e30))
                ms.append((cb * lmat_dt).astype(BF16))
                ecols.append(jnp.exp2(colb))
                bws.append((b_t * w_t[e:e + 1, :]).astype(BF16))
                cds.append(jnp.broadcast_to(cd[:, e:e + 1], (1, LANES)))
            xp = xs[:, col:col + LANES]
            rhs = jnp.concatenate([jnp.where(lo_half, xp, 0.0).astype(BF16),
                                   jnp.where(lo_half, 0.0, xp).astype(BF16)], axis=0)
            lhs = jnp.concatenate(ms, axis=1)
            ydiag = jnp.dot(lhs, rhs, preferred_element_type=F32)
            y_pair = ydiag + jnp.where(lo_half, ecols[0], ecols[1]) * yoff[:, kp * LANES:(kp + 1) * LANES]
            cd_pair = jnp.where(lo_row, cds[0], cds[1])
            ds = jnp.dot(jnp.concatenate(bws, axis=1), rhs, preferred_element_type=F32)
            st_ref[:, col:col + LANES] = st_ref[:, col:col + LANES] * cd_pair + ds
            if final:
                y_tot = y_pair + yf_ref[:, col:col + LANES] + dskip_ref[:, col:col + LANES] * xp
                zz = z_ref[:, col:col + LANES].astype(F32)
                gated = y_tot * (zz * _sigmoid(zz))
                gat_ref[:, col:col + LANES] = gated
                gsq = gsq + gated * gated
            else:
                y_ref[:, col:col + LANES] = y_pair

    if final:
        ms = jnp.sum(gsq, axis=-1, keepdims=True) * (1.0 / SSD_WIDTH)
        y_ref[...] = (gat_ref[...] * lax.rsqrt(ms + EPS) * nw_ref[...]).astype(BF16)

    @pl.when(i == nc - 1)
    def _():
        sfin_ref[...] = st_ref[...]


def _ssd_constants(rev):
    q = SSD_CHUNK
    t = np.arange(q)[:, None]
    r = np.arange(CONV_W * q)[None, :]
    shift_cur = (r % q) == t + r // q - CONV_PAD_LEFT
    shift_top = np.zeros((HALO, 2 * HALO), bool)
    shift_top[0, HALO - 2] = shift_top[1, HALO - 1] = shift_top[0, 2 * HALO - 1] = True
    shift_bot = np.zeros((HALO, HALO), bool)
    shift_bot[HALO - 1, 0] = True
    i = np.arange(q)[:, None]
    j = np.arange(q)[None, :]
    tri = (j >= i) if rev else (j <= i)
    return [jnp.asarray(m, BF16) for m in (shift_cur, shift_top, shift_bot, tri)]


def _ssd_sweep(src, dtraw, s0, dtb, alog, *, rev, col_bc=None, conv=None, extra=None):
    bsz, s, _ = dtraw.shape
    nc = s // SSD_CHUNK
    q = SSD_CHUNK
    hb = q // HALO
    final = extra is not None
    assert rev or not final

    def cidx(i):
        return (nc - 1 - i) if rev else i

    def cur(col_block):
        return lambda b, i: (b, cidx(i), col_block)

    def prev(col_block):
        return lambda b, i: (b, jnp.maximum(cidx(i) * hb - 1, 0), col_block)

    def nxt(col_block):
        return lambda b, i: (b, jnp.minimum(cidx(i) * hb + hb, s // HALO - 1), col_block)

    const2 = lambda b, i: (0, 0)
    xw_, bw_ = SSD_WIDTH, 2 * SSD_BC
    shift_cur, shift_top, shift_bot, tri = _ssd_constants(rev)
    if rev:
        in_specs = [pl.BlockSpec((None, q, xw_), cur(0)), pl.BlockSpec((None, q, bw_), cur(0))]
        args = list(src)
    else:
        p = src
        in_specs = [pl.BlockSpec((None, q, xw_), cur(COL_XS // xw_)),
                    pl.BlockSpec((None, HALO, xw_), prev(COL_XS // xw_)),
                    pl.BlockSpec((None, HALO, xw_), nxt(COL_XS // xw_)),
                    pl.BlockSpec((None, q, bw_), cur(col_bc // bw_)),
                    pl.BlockSpec((None, HALO, bw_), prev(col_bc // bw_)),
                    pl.BlockSpec((None, HALO, bw_), nxt(col_bc // bw_)),
                    pl.BlockSpec((CONV_W, xw_), const2),
                    pl.BlockSpec((1, xw_), const2),
                    pl.BlockSpec((CONV_W, bw_), const2),
                    pl.BlockSpec((1, bw_), const2),
                    pl.BlockSpec((q, CONV_W * q), const2),
                    pl.BlockSpec((HALO, 2 * HALO), const2),
                    pl.BlockSpec((HALO, HALO), const2)]
        args = [p, p, p, p, p, p, *conv, shift_cur, shift_top, shift_bot]
    in_specs += [pl.BlockSpec((None, q, LANES), cur(1 if rev else 0)),
                 pl.BlockSpec((1, LANES), const2),
                 pl.BlockSpec((1, LANES), const2),
                 pl.BlockSpec((q, q), const2),
                 pl.BlockSpec((None, SSD_STATE, xw_), lambda b, i: (b, 0, 0))]
    args += [dtraw, dtb, alog, tri, s0]
    scratch = [pltpu.VMEM((SSD_STATE, xw_), F32)]
    if final:
        yf, z_src, dskip, nw = extra
        in_specs += [pl.BlockSpec((None, q, xw_), cur(0)),
                     pl.BlockSpec((None, q, xw_), cur(COL_Z // xw_)),
                     pl.BlockSpec((1, xw_), const2),
                     pl.BlockSpec((1, xw_), const2)]
        args += [yf, z_src, dskip, nw]
        scratch.append(pltpu.VMEM((q, xw_), F32))
    out_shape = [jax.ShapeDtypeStruct((bsz, s, xw_), BF16 if final else F32),
                 jax.ShapeDtypeStruct((bsz, SSD_STATE, xw_), F32)]
    out_specs = [pl.BlockSpec((None, q, xw_), cur(0)),
                 pl.BlockSpec((None, SSD_STATE, xw_), lambda b, i: (b, 0, 0))]
    if not rev:
        out_shape += [jax.ShapeDtypeStruct((bsz, s, xw_), BF16), jax.ShapeDtypeStruct((bsz, s, bw_), BF16)]
        out_specs += [pl.BlockSpec((None, q, xw_), cur(0)), pl.BlockSpec((None, q, bw_), cur(0))]
    kern = functools.partial(_ssd_kernel, rev=rev, final=final, nc=nc)
    return pl.pallas_call(
        kern,
        out_shape=tuple(out_shape),
        grid=(bsz, nc),
        in_specs=in_specs,
        out_specs=tuple(out_specs),
        scratch_shapes=scratch,
        compiler_params=_params(("arbitrary", "arbitrary")),
        name="ssd_" + ("bwd" if rev else "fwd") + ("_final" if final else ""),
    )(*args)


def _lru_kernel(*refs, rows, width, tb, emit):
    if emit:
        (xr_ref, yr_ref, cw_ref, cb_ref, wg_ref, bg_ref, lam_ref, h0_ref,
         rg_ref, fin_ref,
         xe_ref, af_ref, uf_ref, ab_ref, ub_ref, sh_ref, fl_ref, pe_ref, cinf_ref, cinb_ref) = refs
    else:
        (xr_ref, cw_ref, cb_ref, wg_ref, bg_ref, lam_ref, h0_ref,
         fin_ref,
         xe_ref, af_ref, uf_ref, ab_ref, ub_ref, sh_ref, fl_ref, pe_ref, cinf_ref, cinb_ref) = refs
    R, W = rows, width
    hd = LRU_HEAD_DIM
    rb = tb // W

    xe_ref[2:2 + R] = xr_ref[...].astype(F32).reshape(R, W, hd)
    zero8 = jnp.zeros((8, hd), F32)
    sh_ref[0:8, :] = zero8
    sh_ref[8 + W:16 + W, :] = zero8
    for src, dst, off in ((R - 2, 0, 7), (R - 1, 1, 7), (0, R + 2, 9)):
        sh_ref[8:8 + W, :] = xe_ref[2 + src]
        xe_ref[dst] = sh_ref[off:off + W, :]

    a_scale = _softplus(-lam_ref[...]) * (-0.5 * LRU_C * LOG2E)
    cwv = cw_ref[...]
    cbv = cb_ref[...]
    wg = wg_ref[...]
    bg = bg_ref[...]

    def gate_body(blk, carry):
        r0 = blk * rb
        xc = cbv
        for k in range(CONV_W):
            xc = xc + cwv[k:k + 1, :] * xe_ref[pl.ds(r0 + k, rb)]
        xc2 = xc.reshape(rb * W, hd)
        pre = jnp.dot(xc2.astype(BF16), wg, preferred_element_type=F32) + bg
        half_x = 0.5 * xc2
        for d, (a_ref, u_ref) in enumerate(((af_ref, uf_ref), (ab_ref, ub_ref))):
            t_r = jnp.tanh(pre[:, (2 * d) * hd:(2 * d + 1) * hd])
            t_i = jnp.tanh(pre[:, (2 * d + 1) * hd:(2 * d + 2) * hd])
            k = a_scale[d:d + 1, :]
            a = jnp.exp2(t_r * k + k)
            v = 1.0 - a * a
            u = (v * lax.rsqrt(jnp.maximum(v, 1e-37))) * (t_i * half_x + half_x)
            a_ref[pl.ds(r0, rb)] = a.reshape(rb, W, hd)
            u_ref[pl.ds(r0, rb)] = u.reshape(rb, W, hd)
        return carry

    lax.fori_loop(0, R // rb, gate_body, 0)

    zeros = jnp.zeros((W, hd), F32)
    ones = jnp.ones((W, hd), F32)

    def scan_step(a_ref, u_ref, r, h, p):
        a = a_ref[r]
        h = a * h + u_ref[r]
        p = a * p
        u_ref[r] = h
        a_ref[r] = p
        return h, p

    def scan_body(t, carry):
        hf, pf, hb, pb = carry
        hf, pf = scan_step(af_ref, uf_ref, t, hf, pf)
        hb, pb = scan_step(ab_ref, ub_ref, R - 1 - t, hb, pb)
        return hf, pf, hb, pb

    def carry_chain(h_end, p_end, h0, cin_ref, reverse):
        fl_ref[...] = h_end
        pe_ref[...] = p_end

        def body(t, carry):
            cc = (W - 1 - t) if reverse else t
            cin_ref[pl.ds(cc, 1), :] = carry
            return fl_ref[pl.ds(cc, 1), :] + pe_ref[pl.ds(cc, 1), :] * carry
        return lax.fori_loop(0, W, body, h0)

    hf_end, pf_end, hb_end, pb_end = lax.fori_loop(0, R, scan_body, (zeros, ones, zeros, ones), unroll=4)
    fin_f = carry_chain(hf_end, pf_end, h0_ref[0:1, :], cinf_ref, False)
    fin_b = carry_chain(hb_end, pb_end, h0_ref[1:2, :], cinb_ref, True)
    fin_ref[0:1, :] = fin_f
    fin_ref[1:2, :] = fin_b

    if emit:
        cin_f = cinf_ref[...]
        cin_b = cinb_ref[...]

        def out_body(r, carry):
            h = (uf_ref[r] + af_ref[r] * cin_f) + (ub_ref[r] + ab_ref[r] * cin_b)
            row0 = pl.multiple_of(r * W, W)
            y = yr_ref[pl.ds(row0, W), :].astype(F32)
            rg_ref[pl.ds(row0, W), :] = (h * _gelu_tanh(y)).astype(BF16)
            return carry

        lax.fori_loop(0, R, out_body, 0, unroll=2)


def _lru(xr_src, xr_col, yr_src, yr_col, cw, cb, wg, bg, lam, h0, *, rows, width, emit):
    bsz, s, _ = xr_src.shape
    hd = LRU_HEAD_DIM
    tb = min(512, s)
    kern = functools.partial(_lru_kernel, rows=rows, width=width, tb=tb, emit=emit)
    head = lambda b, h: (0, h)
    in_specs = [pl.BlockSpec((None, s, hd), lambda b, h: (b, 0, xr_col // hd + h))]
    args = [xr_src]
    if emit:
        in_specs.append(pl.BlockSpec((None, s, hd), lambda b, h: (b, 0, yr_col // hd + h)))
        args.append(yr_src)
    in_specs += [pl.BlockSpec((CONV_W, hd), head),
                 pl.BlockSpec((1, hd), head),
                 pl.BlockSpec((None, hd, 4 * hd), lambda b, h: (h, 0, 0)),
                 pl.BlockSpec((None, 1, 4 * hd), lambda b, h: (h, 0, 0)),
                 pl.BlockSpec((2, hd), head),
                 pl.BlockSpec((None, 2, hd), lambda b, h: (b, 0, h))]
    args += [cw, cb, wg, bg, lam, h0]
    out_shape = [jax.ShapeDtypeStruct((bsz, 2, LRU_WIDTH), F32)]
    out_specs = [pl.BlockSpec((None, 2, hd), lambda b, h: (b, 0, h))]
    if emit:
        out_shape.insert(0, jax.ShapeDtypeStruct((bsz, s, LRU_WIDTH), BF16))
        out_specs.insert(0, pl.BlockSpec((None, s, hd), lambda b, h: (b, 0, h)))
    big = pltpu.VMEM((rows, width, hd), F32)
    scratch = [pltpu.VMEM((rows + 3, width, hd), F32), big, big, big, big,
               pltpu.VMEM((width + 16, hd), F32),
               pltpu.VMEM((width, hd), F32), pltpu.VMEM((width, hd), F32),
               pltpu.VMEM((width, hd), F32), pltpu.VMEM((width, hd), F32)]
    return pl.pallas_call(
        kern,
        out_shape=tuple(out_shape),
        grid=(bsz, LRU_HEADS),
        in_specs=in_specs,
        out_specs=tuple(out_specs),
        scratch_shapes=scratch,
        compiler_params=_params(("arbitrary", "arbitrary")),
        name="lru" + ("" if emit else "_ctx"),
    )(*args)


def _merge_a_kernel(yn_ref, rg_ref, ws_ref, wl_ref, gs_ref, gr_ref, m_ref):
    o_s = jnp.dot(yn_ref[...], ws_ref[...], preferred_element_type=F32)
    o_r = jnp.dot(rg_ref[...], wl_ref[...], preferred_element_type=F32)
    m_ref[...] = (gs_ref[...].astype(F32) * o_s + gr_ref[...].astype(F32) * o_r).astype(BF16)


def _merge_a(yn, rg, ws, wl, p, tm):
    bsz, s, d = yn.shape
    row = lambda b, i: (b, i, 0)
    whole = lambda b, i: (0, 0)
    return pl.pallas_call(
        _merge_a_kernel,
        out_shape=jax.ShapeDtypeStruct((bsz, s, d), BF16),
        grid=(bsz, s // tm),
        in_specs=[pl.BlockSpec((None, tm, d), row),
                  pl.BlockSpec((None, tm, d), row),
                  pl.BlockSpec((d, d), whole),
                  pl.BlockSpec((d, d), whole),
                  pl.BlockSpec((None, tm, d), lambda b, i: (b, i, COL_GS // d)),
                  pl.BlockSpec((None, tm, d), lambda b, i: (b, i, COL_GR // d))],
        out_specs=pl.BlockSpec((None, tm, d), row),
        compiler_params=_params(("arbitrary", "arbitrary")),
        name="merge_a",
    )(yn, rg, ws, wl, p, p)


def _merge_b_kernel(m_ref, w_ref, x_ref, gm_ref, sh_ref, sc_ref, g_ref, x1_ref, h_ref):
    mix = jnp.dot(m_ref[...], w_ref[...], preferred_element_type=F32)
    x1 = x_ref[...] + gm_ref[...] * mix
    x1_ref[...] = x1
    h_ref[...] = _norm_mod(x1, g_ref[...], sh_ref[...], sc_ref[...]).astype(BF16)


def _merge_b(m, w_o, x, gm, shift, scale, g, tm):
    bsz, s, d = m.shape
    row = lambda b, i: (b, i, 0)
    vec = lambda b, i: (b, 0, 0)
    return pl.pallas_call(
        _merge_b_kernel,
        out_shape=(jax.ShapeDtypeStruct((bsz, s, d), F32), jax.ShapeDtypeStruct((bsz, s, d), BF16)),
        grid=(bsz, s // tm),
        in_specs=[pl.BlockSpec((None, tm, d), row),
                  pl.BlockSpec((d, d), lambda b, i: (0, 0)),
                  pl.BlockSpec((None, tm, d), row),
                  pl.BlockSpec((None, 1, d), vec),
                  pl.BlockSpec((None, 1, d), vec),
                  pl.BlockSpec((None, 1, d), vec),
                  pl.BlockSpec((1, d), lambda b, i: (0, 0))],
        out_specs=(pl.BlockSpec((None, tm, d), row), pl.BlockSpec((None, tm, d), row)),
        compiler_params=_params(("arbitrary", "arbitrary")),
        name="merge_b",
    )(m, w_o, x, gm, shift, scale, g)


def _ffn_up_kernel(h_ref, w1_ref, w3_ref, a_ref):
    h = h_ref[...]
    gate = jnp.dot(h, w1_ref[...], preferred_element_type=F32)
    up = jnp.dot(h, w3_ref[...], preferred_element_type=F32)
    a_ref[...] = (gate * _sigmoid(gate) * up).astype(BF16)


def _ffn_up(h, w13, tm, tn):
    bsz, s, d = h.shape
    hid = w13.shape[1] // 2
    nb = hid // tn
    return pl.pallas_call(
        _ffn_up_kernel,
        out_shape=jax.ShapeDtypeStruct((bsz, s, hid), BF16),
        grid=(bsz, s // tm, nb),
        in_specs=[pl.BlockSpec((None, tm, d), lambda b, i, n: (b, i, 0)),
                  pl.BlockSpec((d, tn), lambda b, i, n: (0, n)),
                  pl.BlockSpec((d, tn), lambda b, i, n: (0, nb + n))],
        out_specs=pl.BlockSpec((None, tm, tn), lambda b, i, n: (b, i, n)),
        compiler_params=_params(("arbitrary", "arbitrary", "arbitrary")),
        name="ffn_up",
    )(h, w13, w13)


def _ffn_down_kernel(a_ref, w_ref, x_ref, gf_ref, fn_ref, o_ref):
    x2 = x_ref[...] + gf_ref[...] * jnp.dot(a_ref[...], w_ref[...], preferred_element_type=F32)
    ms = jnp.mean(x2 * x2, axis=-1, keepdims=True)
    o_ref[...] = x2 * lax.rsqrt(ms + EPS) * fn_ref[...]


def _ffn_down(a, w2, x1, gf, fnorm, tm):
    bsz, s, hid = a.shape
    d = w2.shape[1]
    row = lambda b, i: (b, i, 0)
    return pl.pallas_call(
        _ffn_down_kernel,
        out_shape=jax.ShapeDtypeStruct((bsz, s, d), F32),
        grid=(bsz, s // tm),
        in_specs=[pl.BlockSpec((None, tm, hid), row),
                  pl.BlockSpec((hid, d), lambda b, i: (0, 0)),
                  pl.BlockSpec((None, tm, d), row),
                  pl.BlockSpec((None, 1, d), lambda b, i: (b, 0, 0)),
                  pl.BlockSpec((1, d), lambda b, i: (0, 0))],
        out_specs=pl.BlockSpec((None, tm, d), row),
        compiler_params=_params(("arbitrary", "arbitrary")),
        name="ffn_down",
    )(a, w2, x1, gf, fnorm)


def _pad_lanes(v, n=LANES):
    return jnp.pad(v, (0, n - v.shape[0])).reshape(1, n)


def _layer(x, ctx, mod, mod_c, prm):
    bsz, seq, d = x.shape
    clen = ctx.shape[1]
    rows = seq // GRID_W

    sh_m, sc_m, g_m, sh_f, sc_f, g_f = [m.reshape(bsz, 1, d) for m in jnp.split(mod, N_MOD, axis=-1)]
    csh_m, csc_m = [jnp.broadcast_to(m.reshape(1, 1, d), (bsz, 1, d))
                    for m in jnp.split(mod_c, N_MOD, axis=-1)[:2]]

    w_in = prm["w_in"]
    o = 0
    parts = {}
    for name, size in (("z", SSD_WIDTH), ("xs", SSD_WIDTH), ("b", SSD_BC), ("c", SSD_BC),
                       ("dt", 2 * SSD_HEADS), ("xr", LRU_WIDTH), ("yr", LRU_WIDTH)):
        parts[name] = w_in[:, o:o + size]
        o += size
    w_cat = jnp.concatenate([parts[k].astype(BF16) for k in ("xs", "xr", "z", "yr")]
                            + [prm["w_gate"].astype(BF16), parts["b"].astype(BF16), parts["c"].astype(BF16)], axis=1)
    b_cat = jnp.concatenate([jnp.zeros((COL_GS,), F32), prm["b_gate"],
                             jnp.zeros((NP - COL_BC,), F32)]).reshape(1, NP)
    w_dt = jnp.zeros((d, DT_COLS), F32)
    w_dt = w_dt.at[:, 0:SSD_HEADS].set(parts["dt"][:, :SSD_HEADS])
    w_dt = w_dt.at[:, LANES:LANES + SSD_HEADS].set(parts["dt"][:, SSD_HEADS:]).astype(BF16)
    g_mix = prm["norm_mix"].reshape(1, d)

    p_ctx, dt_ctx = _in_proj(ctx, csh_m, csc_m, g_mix, w_cat, b_cat, w_dt, tm=clen, ctx_subset=True)
    p_lat, dt_lat = _in_proj(x, sh_m, sc_m, g_mix, w_cat, b_cat, w_dt, *TILES["in_proj"], ctx_subset=False)

    cwx = prm["ssd_conv_w"][:, :SSD_WIDTH]
    cwbc = prm["ssd_conv_w"][:, SSD_WIDTH:]
    cbx = prm["ssd_conv_b"][:SSD_WIDTH].reshape(1, -1)
    cbbc = prm["ssd_conv_b"][SSD_WIDTH:].reshape(1, -1)
    dtb = [_pad_lanes(prm["ssd_dt_bias"][k]) for k in range(2)]
    alog = [_pad_lanes(prm["ssd_a_log"][k]) for k in range(2)]
    conv = (cwx, cbx, cwbc, cbbc)
    zero_state = jnp.zeros((bsz, SSD_STATE, SSD_WIDTH), F32)
    _, s_f, xc_ctx, bc_ctx = _ssd_sweep(p_ctx, dt_ctx, zero_state, dtb[0], alog[0], rev=False,
                                        col_bc=CTX_COL_BC, conv=conv)
    _, s_b = _ssd_sweep((xc_ctx, bc_ctx), dt_ctx, zero_state, dtb[1], alog[1], rev=True)
    y_f, _, xc_lat, bc_lat = _ssd_sweep(p_lat, dt_lat, s_f, dtb[0], alog[0], rev=False, col_bc=COL_BC, conv=conv)
    dskip = jnp.repeat(prm["ssd_d"], SSD_HEAD_DIM).reshape(1, SSD_WIDTH)
    yn, _ = _ssd_sweep((xc_lat, bc_lat), dt_lat, s_b, dtb[1], alog[1], rev=True,
                       extra=(y_f, p_lat, dskip, prm["ssd_norm"].reshape(1, SSD_WIDTH)))

    hd = LRU_HEAD_DIM
    wg = (jnp.concatenate([prm["lru_w_a"][0], prm["lru_w_x"][0],
                           prm["lru_w_a"][1], prm["lru_w_x"][1]], axis=-1) * 0.5).astype(BF16)
    bg = jnp.stack([prm["lru_b_a"][0].reshape(LRU_HEADS, hd), prm["lru_b_x"][0].reshape(LRU_HEADS, hd),
                    prm["lru_b_a"][1].reshape(LRU_HEADS, hd), prm["lru_b_x"][1].reshape(LRU_HEADS, hd)],
                   axis=1).reshape(LRU_HEADS, 1, 4 * hd) * 0.5
    lcw = prm["lru_conv_w"]
    lcb = prm["lru_conv_b"].reshape(1, -1)
    lam = prm["lru_lambda"]
    cw_ = 8
    cr_ = clen // cw_
    xr_ctx = p_ctx[:, :, COL_XR:COL_XR + LRU_WIDTH]
    xr_ctx = xr_ctx.reshape(bsz, cw_, cr_, LRU_WIDTH).transpose(0, 2, 1, 3).reshape(bsz, clen, LRU_WIDTH)
    zero_h = jnp.zeros((bsz, 2, LRU_WIDTH), F32)
    (f_ctx,) = _lru(xr_ctx, 0, None, 0, lcw, lcb, wg, bg, lam, zero_h, rows=cr_, width=cw_, emit=False)
    rg, _ = _lru(p_lat, COL_XR, p_lat, COL_YR, lcw, lcb, wg, bg, lam, f_ctx,
                 rows=rows, width=GRID_W, emit=True)

    m = _merge_a(yn, rg, prm["w_out_ssd"].astype(BF16), prm["w_out_lru"].astype(BF16), p_lat, *TILES["merge_a"])
    x1, h_ffn = _merge_b(m, prm["w_o"].astype(BF16), x, g_m, sh_f, sc_f, prm["norm_ffn"].reshape(1, d),
                         *TILES["merge_b"])

    act = _ffn_up(h_ffn, prm["ffn_w13"].astype(BF16), *TILES["ffn_up"])
    return _ffn_down(act, prm["ffn_w2"].astype(BF16), x1, g_f, prm["final_norm"].reshape(1, d), *TILES["ffn_down"])


def kernel(x, c, ctx, c_ctx, w_ada, b_ada, norm_mix, norm_ffn, w_in, ssd_conv_w, ssd_conv_b, ssd_dt_bias,
           ssd_a_log, ssd_d, ssd_norm, w_out_ssd, lru_conv_w, lru_conv_b, lru_w_a, lru_b_a, lru_w_x, lru_b_x,
           lru_lambda, w_out_lru, w_gate, b_gate, w_o, ffn_w13, ffn_w2, final_norm):
    assert w_ada.shape[0] == 1, "single-layer trunk"
    bsz = x.shape[0]
    cvec = jnp.zeros((8, D_MODEL), F32).at[:bsz].set(c).at[bsz].set(c_ctx)
    mod_all = _ada(cvec, w_ada[0], b_ada[0].reshape(1, -1))
    prm = dict(w_in=w_in[0], ssd_conv_w=ssd_conv_w[0], ssd_conv_b=ssd_conv_b[0], ssd_dt_bias=ssd_dt_bias[0],
               ssd_a_log=ssd_a_log[0], ssd_d=ssd_d[0], ssd_norm=ssd_norm[0], w_out_ssd=w_out_ssd[0],
               lru_conv_w=lru_conv_w[0], lru_conv_b=lru_conv_b[0], lru_w_a=lru_w_a[0], lru_b_a=lru_b_a[0],
               lru_w_x=lru_w_x[0], lru_b_x=lru_b_x[0], lru_lambda=lru_lambda[0], w_out_lru=w_out_lru[0],
               w_gate=w_gate[0], b_gate=b_gate[0], w_o=w_o[0], norm_mix=norm_mix[0], norm_ffn=norm_ffn[0],
               ffn_w13=ffn_w13[0], ffn_w2=ffn_w2[0], final_norm=final_norm)
    return _layer(x, ctx, mod_all[:bsz], mod_all[bsz], prm)
```

```python
import functools

import jax
import numpy as np
import jax.numpy as jnp
from jax import lax
from jax.experimental import pallas as pl
from jax.experimental.pallas import tpu as pltpu

F32 = jnp.float32
BF16 = jnp.bfloat16

D_MODEL = 2048
GRID_W = 64
N_MOD = 6
SSD_HEAD_DIM = 64
SSD_HEADS = 32
SSD_WIDTH = SSD_HEADS * SSD_HEAD_DIM
SSD_GROUPS = 4
SSD_HPG = SSD_HEADS // SSD_GROUPS
SSD_STATE = 128
SSD_BC = SSD_GROUPS * SSD_STATE
SSD_CHUNK = 128
CONV_W = 4
CONV_PAD_LEFT = 2
LRU_WIDTH = D_MODEL
LRU_HEADS = 16
LRU_HEAD_DIM = LRU_WIDTH // LRU_HEADS
LRU_C = 8.0
EPS = 1e-6

LANES = 128
MXU_COLS = 256
LRU_CTX_HEADS_PER_STEP = 8
HALO = 16
VMEM_LIMIT = 56 * 1024 * 1024

COL_XS = 0
COL_XR = 2048
COL_Z = 4096
COL_YR = 6144
COL_GS = 8192
COL_GR = 10240
COL_BC = 12288
NP = 13312
CTX_COL_BC = 4096
CTX_NP = 5120
DT_COLS = 256


TILES = {"in_proj": (1024,), "merge_a": (512,), "merge_b": (512,), "ffn_up": (1024, 512), "ffn_down": (256,)}


def _params(sem, vmem=VMEM_LIMIT):
    return pltpu.CompilerParams(dimension_semantics=sem, vmem_limit_bytes=vmem)


LOG2E = 1.4426950408889634


def _sigmoid(x):
    return 1.0 / (1.0 + jnp.exp2(x * (-LOG2E)))


def _softplus(x):
    return jnp.maximum(x, 0.0) + jnp.log1p(jnp.exp(-jnp.abs(x)))


def _gelu_tanh(x):
    c1 = 0.7978845608028654
    half_x = 0.5 * x
    return half_x * jnp.tanh(x * (c1 + (c1 * 0.044715) * (x * x))) + half_x


def _norm_mod(x, g, shift, scale):
    xf = x.astype(F32)
    ms = jnp.mean(xf * xf, axis=-1, keepdims=True)
    hn = xf * lax.rsqrt(ms + EPS) * g
    return hn * (1.0 + scale) + shift


def _ada_kernel(c_ref, w_ref, b_ref, o_ref):
    c = c_ref[...]
    s = (c * _sigmoid(c)).astype(BF16)
    o_ref[...] = jnp.dot(s, w_ref[...].astype(BF16), preferred_element_type=F32) + b_ref[...]


def _ada(cvec, w, b):
    m, d = cvec.shape
    n = w.shape[1]
    tn = 1024
    return pl.pallas_call(
        _ada_kernel,
        out_shape=jax.ShapeDtypeStruct((m, n), F32),
        grid=(n // tn,),
        in_specs=[pl.BlockSpec((m, d), lambda j: (0, 0)),
                  pl.BlockSpec((d, tn), lambda j: (0, j)),
                  pl.BlockSpec((1, tn), lambda j: (0, j))],
        out_specs=pl.BlockSpec((m, tn), lambda j: (0, j)),
        compiler_params=_params(("arbitrary",)),
        name="ada",
    )(cvec, w, b)


def _in_proj_kernel(x_ref, sh_ref, sc_ref, g_ref, w_ref, b_ref, wdt_ref, p_ref, dt_ref, hn_ref,
                    *, gate_lo, gate_hi):
    n = pl.program_id(2)

    @pl.when(n == 0)
    def _():
        h = _norm_mod(x_ref[...], g_ref[...], sh_ref[...], sc_ref[...]).astype(BF16)
        hn_ref[...] = h
        dt_ref[...] = jnp.dot(h, wdt_ref[...], preferred_element_type=F32)

    def project(gate):
        hn = hn_ref[...]
        for j in range(0, p_ref.shape[1], MXU_COLS):
            acc = jnp.dot(hn, w_ref[:, j:j + MXU_COLS], preferred_element_type=F32) + b_ref[:, j:j + MXU_COLS]
            p_ref[:, j:j + MXU_COLS] = (_sigmoid(acc) if gate else acc).astype(BF16)

    if gate_lo == gate_hi:
        project(False)
        return
    is_gate = jnp.logical_and(n >= gate_lo, n < gate_hi)
    pl.when(is_gate)(functools.partial(project, True))
    pl.when(jnp.logical_not(is_gate))(functools.partial(project, False))


def _in_proj(x, shift, scale, g, w_cat, b_cat, w_dt, tm, ctx_subset):
    bsz, s, d = x.shape
    tn = 1024
    if ctx_subset:
        n_out = CTX_NP
        gates = (0, 0)
        wtile = lambda n: jnp.where(n < CTX_COL_BC // tn, n, COL_BC // tn)
    else:
        n_out = NP
        gates = (COL_GS // tn, COL_BC // tn)
        wtile = lambda n: n
    kern = functools.partial(_in_proj_kernel, gate_lo=gates[0], gate_hi=gates[1])
    return pl.pallas_call(
        kern,
        out_shape=(jax.ShapeDtypeStruct((bsz, s, n_out), BF16),
                   jax.ShapeDtypeStruct((bsz, s, DT_COLS), F32)),
        grid=(bsz, s // tm, n_out // tn),
        in_specs=[pl.BlockSpec((None, tm, d), lambda b, i, n: (b, i, 0)),
                  pl.BlockSpec((None, 1, d), lambda b, i, n: (b, 0, 0)),
                  pl.BlockSpec((None, 1, d), lambda b, i, n: (b, 0, 0)),
                  pl.BlockSpec((1, d), lambda b, i, n: (0, 0)),
                  pl.BlockSpec((d, tn), lambda b, i, n: (0, wtile(n))),
                  pl.BlockSpec((1, tn), lambda b, i, n: (0, wtile(n))),
                  pl.BlockSpec((d, DT_COLS), lambda b, i, n: (0, 0))],
        out_specs=(pl.BlockSpec((None, tm, tn), lambda b, i, n: (b, i, n)),
                   pl.BlockSpec((None, tm, DT_COLS), lambda b, i, n: (b, i, 0))),
        scratch_shapes=[pltpu.VMEM((tm, d), BF16)],
        compiler_params=_params(("arbitrary", "arbitrary", "arbitrary")),
        name="in_proj",
    )(x, shift, scale, g, w_cat, b_cat, w_dt)


def _split3(v):
    v1 = v.astype(BF16)
    r1 = v - v1.astype(F32)
    v2 = r1.astype(BF16)
    r2 = r1 - v2.astype(F32)
    return v1, v2, r2.astype(BF16)


def _ssd_kernel(*refs, rev, final, nc):
    it = iter(refs)
    if rev:
        xsc_ref, bcc_ref = next(it), next(it)
    else:
        xs_ref, xsp_ref, xsn_ref, bc_ref, bcp_ref, bcn_ref = [next(it) for _ in range(6)]
        cwx_ref, cbx_ref, cwbc_ref, cbbc_ref, shc_ref, sht_ref, shb_ref = [next(it) for _ in range(7)]
    dt_ref, dtb_ref, alog_ref, tri_ref, s0_ref = [next(it) for _ in range(5)]
    if final:
        yf_ref, z_ref, dskip_ref, nw_ref = [next(it) for _ in range(4)]
    y_ref, sfin_ref = next(it), next(it)
    if not rev:
        xsc_out_ref, bcc_out_ref = next(it), next(it)
    st_ref = next(it)
    if final:
        gat_ref = next(it)

    i = pl.program_id(1)
    c = (nc - 1 - i) if rev else i
    q = SSD_CHUNK

    @pl.when(i == 0)
    def _():
        st_ref[...] = s0_ref[...]

    tri = tri_ref[...]

    if rev:
        xs = xsc_ref[...].astype(F32)
        bcv = bcc_ref[...].astype(F32)
    else:
        has_prev = c > 0
        has_next = c < nc - 1
        shift_cur = shc_ref[...]
        shift_top = sht_ref[...]
        shift_bot = shb_ref[...]

        def conv_silu(cur_ref, prev_ref, next_ref, w_ref, b_ref):
            wb = [w_ref[k:k + 1, :].astype(BF16) for k in range(CONV_W)]
            cur = cur_ref[...]
            taps = jnp.concatenate([cur * wb[k] for k in range(CONV_W)], axis=0)
            out = jnp.dot(shift_cur, taps, preferred_element_type=F32) + b_ref[...]
            prev = jnp.where(has_prev, prev_ref[...], jnp.zeros_like(prev_ref))
            top = jnp.concatenate([prev * wb[0], prev * wb[1]], axis=0)
            nxt_rows = jnp.where(has_next, next_ref[...], jnp.zeros_like(next_ref))
            bot = nxt_rows * wb[CONV_W - 1]
            head = out[0:HALO] + jnp.dot(shift_top, top, preferred_element_type=F32)
            tail = out[q - HALO:q] + jnp.dot(shift_bot, bot, preferred_element_type=F32)
            out = jnp.concatenate([head, out[HALO:q - HALO], tail], axis=0)
            return out * _sigmoid(out)

        xs = conv_silu(xs_ref, xsp_ref, xsn_ref, cwx_ref, cbx_ref)
        bcv = conv_silu(bc_ref, bcp_ref, bcn_ref, cwbc_ref, cbbc_ref)
        xsc_out_ref[...] = xs.astype(BF16)
        bcc_out_ref[...] = bcv.astype(BF16)

    dt = _softplus(dt_ref[...] + dtb_ref[...])
    a = dt * (-jnp.exp(alog_ref[...]))

    ii = lax.broadcasted_iota(jnp.int32, (q, q), 0)
    jj = lax.broadcasted_iota(jnp.int32, (q, q), 1)
    mask = (jj >= ii) if rev else (jj <= ii)

    acum = sum(jnp.dot(tri, p, preferred_element_type=F32) for p in _split3(a))
    a_t = a.T
    dt_t = dt.T
    nt = (((1,), (1,)), ((), ()))
    acum_t = sum(lax.dot_general(p, tri, nt, preferred_element_type=F32) for p in _split3(a_t))

    edge = 0 if rev else q - 1
    total = acum[edge:edge + 1, :]
    cd = jnp.exp(total)
    w_t = dt_t * jnp.exp(acum_t[:, edge:edge + 1] - acum_t)
    acum2 = acum * LOG2E
    adj2_t = acum_t * LOG2E - jnp.log2(dt_t)

    lane = lax.broadcasted_iota(jnp.int32, (q, LANES), 1)
    lo_half = lane < SSD_HEAD_DIM
    lo_row = lo_half[0:1, :]

    gsq = jnp.zeros((q, LANES), F32)
    for g in range(SSD_GROUPS):
        b_g = bcv[:, g * SSD_STATE:(g + 1) * SSD_STATE]
        c_g = bcv[:, SSD_BC + g * SSD_STATE:SSD_BC + (g + 1) * SSD_STATE]
        b_gb = b_g.astype(BF16)
        c_gb = c_g.astype(BF16)
        cb = lax.dot_general(c_gb, b_gb, nt, preferred_element_type=F32)
        gw = SSD_HPG * SSD_HEAD_DIM
        s_in = st_ref[:, g * gw:(g + 1) * gw].astype(BF16)
        yoff = jnp.dot(c_gb, s_in, preferred_element_type=F32)
        b_t = b_g.T
        for kp in range(SSD_HPG // 2):
            pair = g * (SSD_HPG // 2) + kp
            col = pair * LANES
            ms, ecols, bws, cds = [], [], [], []
            for e in (2 * pair, 2 * pair + 1):
                colb = jnp.broadcast_to(acum2[:, e:e + 1], (q, q))
                seg = colb - adj2_t[e:e + 1, :]
                lmat_dt = jnp.exp2(jnp.where(mask, seg, -1e30))
                ms.append((cb * lmat_dt).astype(BF16))
                ecols.append(jnp.exp2(colb))
                bws.append((b_t * w_t[e:e + 1, :]).astype(BF16))
                cds.append(jnp.broadcast_to(cd[:, e:e + 1], (1, LANES)))
            xp = xs[:, col:col + LANES]
            rhs = jnp.concatenate([jnp.where(lo_half, xp, 0.0).astype(BF16),
                                   jnp.where(lo_half, 0.0, xp).astype(BF16)], axis=0)
            lhs = jnp.concatenate(ms, axis=1)
            ydiag = jnp.dot(lhs, rhs, preferred_element_type=F32)
            y_pair = ydiag + jnp.where(lo_half, ecols[0], ecols[1]) * yoff[:, kp * LANES:(kp + 1) * LANES]
            cd_pair = jnp.where(lo_row, cds[0], cds[1])
            ds = jnp.dot(jnp.concatenate(bws, axis=1), rhs, preferred_element_type=F32)
            st_ref[:, col:col + LANES] = st_ref[:, col:col + LANES] * cd_pair + ds
            if final:
                y_tot = y_pair + yf_ref[:, col:col + LANES] + dskip_ref[:, col:col + LANES] * xp
                zz = z_ref[:, col:col + LANES].astype(F32)
                gated = y_tot * (zz * _sigmoid(zz))
                gat_ref[:, col:col + LANES] = gated
                gsq = gsq + gated * gated
            else:
                y_ref[:, col:col + LANES] = y_pair

    if final:
        ms = jnp.sum(gsq, axis=-1, keepdims=True) * (1.0 / SSD_WIDTH)
        y_ref[...] = (gat_ref[...] * lax.rsqrt(ms + EPS) * nw_ref[...]).astype(BF16)

    @pl.when(i == nc - 1)
    def _():
        sfin_ref[...] = st_ref[...]


def _ssd_constants(rev):
    q = SSD_CHUNK
    t = np.arange(q)[:, None]
    r = np.arange(CONV_W * q)[None, :]
    shift_cur = (r % q) == t + r // q - CONV_PAD_LEFT
    shift_top = np.zeros((HALO, 2 * HALO), bool)
    shift_top[0, HALO - 2] = shift_top[1, HALO - 1] = shift_top[0, 2 * HALO - 1] = True
    shift_bot = np.zeros((HALO, HALO), bool)
    shift_bot[HALO - 1, 0] = True
    i = np.arange(q)[:, None]
    j = np.arange(q)[None, :]
    tri = (j >= i) if rev else (j <= i)
    return [jnp.asarray(m, BF16) for m in (shift_cur, shift_top, shift_bot, tri)]


def _ssd_sweep(src, dtraw, s0, dtb, alog, *, rev, col_bc=None, conv=None, extra=None):
    bsz, s, _ = dtraw.shape
    nc = s // SSD_CHUNK
    q = SSD_CHUNK
    hb = q // HALO
    final = extra is not None
    assert rev or not final

    def cidx(i):
        return (nc - 1 - i) if rev else i

    def cur(col_block):
        return lambda b, i: (b, cidx(i), col_block)

    def prev(col_block):
        return lambda b, i: (b, jnp.maximum(cidx(i) * hb - 1, 0), col_block)

    def nxt(col_block):
        return lambda b, i: (b, jnp.minimum(cidx(i) * hb + hb, s // HALO - 1), col_block)

    const2 = lambda b, i: (0, 0)
    xw_, bw_ = SSD_WIDTH, 2 * SSD_BC
    shift_cur, shift_top, shift_bot, tri = _ssd_constants(rev)
    if rev:
        in_specs = [pl.BlockSpec((None, q, xw_), cur(0)), pl.BlockSpec((None, q, bw_), cur(0))]
        args = list(src)
    else:
        p = src
        in_specs = [pl.BlockSpec((None, q, xw_), cur(COL_XS // xw_)),
                    pl.BlockSpec((None, HALO, xw_), prev(COL_XS // xw_)),
                    pl.BlockSpec((None, HALO, xw_), nxt(COL_XS // xw_)),
                    pl.BlockSpec((None, q, bw_), cur(col_bc // bw_)),
                    pl.BlockSpec((None, HALO, bw_), prev(col_bc // bw_)),
                    pl.BlockSpec((None, HALO, bw_), nxt(col_bc // bw_)),
                    pl.BlockSpec((CONV_W, xw_), const2),
                    pl.BlockSpec((1, xw_), const2),
                    pl.BlockSpec((CONV_W, bw_), const2),
                    pl.BlockSpec((1, bw_), const2),
                    pl.BlockSpec((q, CONV_W * q), const2),
                    pl.BlockSpec((HALO, 2 * HALO), const2),
                    pl.BlockSpec((HALO, HALO), const2)]
        args = [p, p, p, p, p, p, *conv, shift_cur, shift_top, shift_bot]
    in_specs += [pl.BlockSpec((None, q, LANES), cur(1 if rev else 0)),
                 pl.BlockSpec((1, LANES), const2),
                 pl.BlockSpec((1, LANES), const2),
                 pl.BlockSpec((q, q), const2),
                 pl.BlockSpec((None, SSD_STATE, xw_), lambda b, i: (b, 0, 0))]
    args += [dtraw, dtb, alog, tri, s0]
    scratch = [pltpu.VMEM((SSD_STATE, xw_), F32)]
    if final:
        yf, z_src, dskip, nw = extra
        in_specs += [pl.BlockSpec((None, q, xw_), cur(0)),
                     pl.BlockSpec((None, q, xw_), cur(COL_Z // xw_)),
                     pl.BlockSpec((1, xw_), const2),
                     pl.BlockSpec((1, xw_), const2)]
        args += [yf, z_src, dskip, nw]
        scratch.append(pltpu.VMEM((q, xw_), F32))
    out_shape = [jax.ShapeDtypeStruct((bsz, s, xw_), BF16 if final else F32),
                 jax.ShapeDtypeStruct((bsz, SSD_STATE, xw_), F32)]
    out_specs = [pl.BlockSpec((None, q, xw_), cur(0)),
                 pl.BlockSpec((None, SSD_STATE, xw_), lambda b, i: (b, 0, 0))]
    if not rev:
        out_shape += [jax.ShapeDtypeStruct((bsz, s, xw_), BF16), jax.ShapeDtypeStruct((bsz, s, bw_), BF16)]
        out_specs += [pl.BlockSpec((None, q, xw_), cur(0)), pl.BlockSpec((None, q, bw_), cur(0))]
    kern = functools.partial(_ssd_kernel, rev=rev, final=final, nc=nc)
    return pl.pallas_call(
        kern,
        out_shape=tuple(out_shape),
        grid=(bsz, nc),
        in_specs=in_specs,
        out_specs=tuple(out_specs),
        scratch_shapes=scratch,
        compiler_params=_params(("arbitrary", "arbitrary")),
        name="ssd_" + ("bwd" if rev else "fwd") + ("_final" if final else ""),
    )(*args)


def _lru_kernel(*refs, rows, width, tb, emit, hps):
    n_in = 8 if emit else 7
    n_out = 2 if emit else 1
    per_head_leading = (4, 5) if emit else (3, 4)
    for hh in range(hps):
        lanes = slice(hh * LRU_HEAD_DIM, (hh + 1) * LRU_HEAD_DIM)
        views = [r.at[hh] if k in per_head_leading else r.at[:, lanes] for k, r in enumerate(refs[:n_in + n_out])]
        _lru_head(*views, *refs[n_in + n_out:], rows=rows, width=width, tb=tb, emit=emit)


def _lru_head(*refs, rows, width, tb, emit):
    if emit:
        (xr_ref, yr_ref, cw_ref, cb_ref, wg_ref, bg_ref, lam_ref, h0_ref,
         rg_ref, fin_ref,
         xe_ref, af_ref, uf_ref, ab_ref, ub_ref, sh_ref, fl_ref, pe_ref, cinf_ref, cinb_ref) = refs
    else:
        (xr_ref, cw_ref, cb_ref, wg_ref, bg_ref, lam_ref, h0_ref,
         fin_ref,
         xe_ref, af_ref, uf_ref, ab_ref, ub_ref, sh_ref, fl_ref, pe_ref, cinf_ref, cinb_ref) = refs
    R, W = rows, width
    hd = LRU_HEAD_DIM
    rb = tb // W

    xe_ref[2:2 + R] = xr_ref[...].astype(F32).reshape(R, W, hd)
    zero8 = jnp.zeros((8, hd), F32)
    sh_ref[0:8, :] = zero8
    sh_ref[8 + W:16 + W, :] = zero8
    for src, dst, off in ((R - 2, 0, 7), (R - 1, 1, 7), (0, R + 2, 9)):
        sh_ref[8:8 + W, :] = xe_ref[2 + src]
        xe_ref[dst] = sh_ref[off:off + W, :]

    a_scale = _softplus(-lam_ref[...]) * (-0.5 * LRU_C * LOG2E)
    cwv = cw_ref[...]
    cbv = cb_ref[...]
    wg = wg_ref[...]
    bg = bg_ref[...]

    def gate_body(blk, carry):
        r0 = blk * rb
        xc = cbv
        for k in range(CONV_W):
            xc = xc + cwv[k:k + 1, :] * xe_ref[pl.ds(r0 + k, rb)]
        xc2 = xc.reshape(rb * W, hd)
        pre = jnp.dot(xc2.astype(BF16), wg, preferred_element_type=F32) + bg
        half_x = 0.5 * xc2
        for d, (a_ref, u_ref) in enumerate(((af_ref, uf_ref), (ab_ref, ub_ref))):
            t_r = jnp.tanh(pre[:, (2 * d) * hd:(2 * d + 1) * hd])
            t_i = jnp.tanh(pre[:, (2 * d + 1) * hd:(2 * d + 2) * hd])
            k = a_scale[d:d + 1, :]
            a = jnp.exp2(t_r * k + k)
            v = 1.0 - a * a
            u = (v * lax.rsqrt(jnp.maximum(v, 1e-37))) * (t_i * half_x + half_x)
            a_ref[pl.ds(r0, rb)] = a.reshape(rb, W, hd)
            u_ref[pl.ds(r0, rb)] = u.reshape(rb, W, hd)
        return carry

    lax.fori_loop(0, R // rb, gate_body, 0)

    zeros = jnp.zeros((W, hd), F32)
    ones = jnp.ones((W, hd), F32)

    def scan_step(a_ref, u_ref, r, h, p):
        a = a_ref[r]
        h = a * h + u_ref[r]
        p = a * p
        u_ref[r] = h
        a_ref[r] = p
        return h, p

    def scan_body(t, carry):
        hf, pf, hb, pb = carry
        hf, pf = scan_step(af_ref, uf_ref, t, hf, pf)
        hb, pb = scan_step(ab_ref, ub_ref, R - 1 - t, hb, pb)
        return hf, pf, hb, pb

    def carry_chain(h_end, p_end, h0, cin_ref, reverse):
        fl_ref[...] = h_end
        pe_ref[...] = p_end

        def body(t, carry):
            cc = (W - 1 - t) if reverse else t
            cin_ref[pl.ds(cc, 1), :] = carry
            return fl_ref[pl.ds(cc, 1), :] + pe_ref[pl.ds(cc, 1), :] * carry
        return lax.fori_loop(0, W, body, h0)

    hf_end, pf_end, hb_end, pb_end = lax.fori_loop(0, R, scan_body, (zeros, ones, zeros, ones), unroll=4)
    fin_f = carry_chain(hf_end, pf_end, h0_ref[0:1, :], cinf_ref, False)
    fin_b = carry_chain(hb_end, pb_end, h0_ref[1:2, :], cinb_ref, True)
    fin_ref[0:1, :] = fin_f
    fin_ref[1:2, :] = fin_b

    if emit:
        cin_f = cinf_ref[...]
        cin_b = cinb_ref[...]

        def out_body(r, carry):
            h = (uf_ref[r] + af_ref[r] * cin_f) + (ub_ref[r] + ab_ref[r] * cin_b)
            row0 = pl.multiple_of(r * W, W)
            y = yr_ref[pl.ds(row0, W), :].astype(F32)
            rg_ref[pl.ds(row0, W), :] = (h * _gelu_tanh(y)).astype(BF16)
            return carry

        lax.fori_loop(0, R, out_body, 0, unroll=2)


def _lru(xr_src, xr_col, yr_src, yr_col, cw, cb, wg, bg, lam, h0, *, rows, width, emit, hps=1):
    bsz, s, _ = xr_src.shape
    hd = hps * LRU_HEAD_DIM
    tb = min(512, s)
    kern = functools.partial(_lru_kernel, rows=rows, width=width, tb=tb, emit=emit, hps=hps)
    head = lambda b, h: (0, h)
    in_specs = [pl.BlockSpec((None, s, hd), lambda b, h: (b, 0, xr_col // hd + h))]
    args = [xr_src]
    if emit:
        in_specs.append(pl.BlockSpec((None, s, hd), lambda b, h: (b, 0, yr_col // hd + h)))
        args.append(yr_src)
    in_specs += [pl.BlockSpec((CONV_W, hd), head),
                 pl.BlockSpec((1, hd), head),
                 pl.BlockSpec((hps, LRU_HEAD_DIM, 4 * LRU_HEAD_DIM), lambda b, h: (h, 0, 0)),
                 pl.BlockSpec((hps, 1, 4 * LRU_HEAD_DIM), lambda b, h: (h, 0, 0)),
                 pl.BlockSpec((2, hd), head),
                 pl.BlockSpec((None, 2, hd), lambda b, h: (b, 0, h))]
    args += [cw, cb, wg, bg, lam, h0]
    out_shape = [jax.ShapeDtypeStruct((bsz, 2, LRU_WIDTH), F32)]
    out_specs = [pl.BlockSpec((None, 2, hd), lambda b, h: (b, 0, h))]
    if emit:
        out_shape.insert(0, jax.ShapeDtypeStruct((bsz, s, LRU_WIDTH), BF16))
        out_specs.insert(0, pl.BlockSpec((None, s, hd), lambda b, h: (b, 0, h)))
    hl = LRU_HEAD_DIM
    big = pltpu.VMEM((rows, width, hl), F32)
    scratch = [pltpu.VMEM((rows + 3, width, hl), F32), big, big, big, big,
               pltpu.VMEM((width + 16, hl), F32),
               pltpu.VMEM((width, hl), F32), pltpu.VMEM((width, hl), F32),
               pltpu.VMEM((width, hl), F32), pltpu.VMEM((width, hl), F32)]
    return pl.pallas_call(
        kern,
        out_shape=tuple(out_shape),
        grid=(bsz, LRU_HEADS // hps),
        in_specs=in_specs,
        out_specs=tuple(out_specs),
        scratch_shapes=scratch,
        compiler_params=_params(("arbitrary", "arbitrary")),
        name="lru" + ("" if emit else "_ctx"),
    )(*args)


def _merge_a_kernel(yn_ref, rg_ref, ws_ref, wl_ref, gs_ref, gr_ref, m_ref):
    yn = yn_ref[...]
    rg = rg_ref[...]
    for j in range(0, m_ref.shape[1], MXU_COLS):
        cols = slice(j, j + MXU_COLS)
        o_s = jnp.dot(yn, ws_ref[:, cols], preferred_element_type=F32)
        o_r = jnp.dot(rg, wl_ref[:, cols], preferred_element_type=F32)
        m_ref[:, cols] = (gs_ref[:, cols].astype(F32) * o_s + gr_ref[:, cols].astype(F32) * o_r).astype(BF16)


def _merge_a(yn, rg, ws, wl, p, tm):
    bsz, s, d = yn.shape
    row = lambda b, i: (b, i, 0)
    whole = lambda b, i: (0, 0)
    return pl.pallas_call(
        _merge_a_kernel,
        out_shape=jax.ShapeDtypeStruct((bsz, s, d), BF16),
        grid=(bsz, s // tm),
        in_specs=[pl.BlockSpec((None, tm, d), row),
                  pl.BlockSpec((None, tm, d), row),
                  pl.BlockSpec((d, d), whole),
                  pl.BlockSpec((d, d), whole),
                  pl.BlockSpec((None, tm, d), lambda b, i: (b, i, COL_GS // d)),
                  pl.BlockSpec((None, tm, d), lambda b, i: (b, i, COL_GR // d))],
        out_specs=pl.BlockSpec((None, tm, d), row),
        compiler_params=_params(("arbitrary", "arbitrary")),
        name="merge_a",
    )(yn, rg, ws, wl, p, p)


def _merge_b_kernel(m_ref, w_ref, x_ref, gm_ref, sh_ref, sc_ref, g_ref, x1_ref, h_ref):
    mix = jnp.dot(m_ref[...], w_ref[...], preferred_element_type=F32)
    x1 = x_ref[...] + gm_ref[...] * mix
    x1_ref[...] = x1
    h_ref[...] = _norm_mod(x1, g_ref[...], sh_ref[...], sc_ref[...]).astype(BF16)


def _merge_b(m, w_o, x, gm, shift, scale, g, tm):
    bsz, s, d = m.shape
    row = lambda b, i: (b, i, 0)
    vec = lambda b, i: (b, 0, 0)
    return pl.pallas_call(
        _merge_b_kernel,
        out_shape=(jax.ShapeDtypeStruct((bsz, s, d), F32), jax.ShapeDtypeStruct((bsz, s, d), BF16)),
        grid=(bsz, s // tm),
        in_specs=[pl.BlockSpec((None, tm, d), row),
                  pl.BlockSpec((d, d), lambda b, i: (0, 0)),
                  pl.BlockSpec((None, tm, d), row),
                  pl.BlockSpec((None, 1, d), vec),
                  pl.BlockSpec((None, 1, d), vec),
                  pl.BlockSpec((None, 1, d), vec),
                  pl.BlockSpec((1, d), lambda b, i: (0, 0))],
        out_specs=(pl.BlockSpec((None, tm, d), row), pl.BlockSpec((None, tm, d), row)),
        compiler_params=_params(("arbitrary", "arbitrary")),
        name="merge_b",
    )(m, w_o, x, gm, shift, scale, g)


def _ffn_up_kernel(h_ref, w1_ref, w3_ref, a_ref):
    h = h_ref[...]
    for j in range(0, a_ref.shape[1], MXU_COLS):
        cols = slice(j, j + MXU_COLS)
        gate = jnp.dot(h, w1_ref[:, cols], preferred_element_type=F32)
        up = jnp.dot(h, w3_ref[:, cols], preferred_element_type=F32)
        a_ref[:, cols] = (gate * _sigmoid(gate) * up).astype(BF16)


def _ffn_up(h, w13, tm, tn):
    bsz, s, d = h.shape
    hid = w13.shape[1] // 2
    nb = hid // tn
    return pl.pallas_call(
        _ffn_up_kernel,
        out_shape=jax.ShapeDtypeStruct((bsz, s, hid), BF16),
        grid=(bsz, s // tm, nb),
        in_specs=[pl.BlockSpec((None, tm, d), lambda b, i, n: (b, i, 0)),
                  pl.BlockSpec((d, tn), lambda b, i, n: (0, n)),
                  pl.BlockSpec((d, tn), lambda b, i, n: (0, nb + n))],
        out_specs=pl.BlockSpec((None, tm, tn), lambda b, i, n: (b, i, n)),
        compiler_params=_params(("arbitrary", "arbitrary", "arbitrary")),
        name="ffn_up",
    )(h, w13, w13)


def _ffn_down_kernel(a_ref, w_ref, x_ref, gf_ref, fn_ref, o_ref):
    x2 = x_ref[...] + gf_ref[...] * jnp.dot(a_ref[...], w_ref[...], preferred_element_type=F32)
    ms = jnp.mean(x2 * x2, axis=-1, keepdims=True)
    o_ref[...] = x2 * lax.rsqrt(ms + EPS) * fn_ref[...]


def _ffn_down(a, w2, x1, gf, fnorm, tm):
    bsz, s, hid = a.shape
    d = w2.shape[1]
    row = lambda b, i: (b, i, 0)
    return pl.pallas_call(
        _ffn_down_kernel,
        out_shape=jax.ShapeDtypeStruct((bsz, s, d), F32),
        grid=(bsz, s // tm),
        in_specs=[pl.BlockSpec((None, tm, hid), row),
                  pl.BlockSpec((hid, d), lambda b, i: (0, 0)),
                  pl.BlockSpec((None, tm, d), row),
                  pl.BlockSpec((None, 1, d), lambda b, i: (b, 0, 0)),
                  pl.BlockSpec((1, d), lambda b, i: (0, 0))],
        out_specs=pl.BlockSpec((None, tm, d), row),
        compiler_params=_params(("arbitrary", "arbitrary")),
        name="ffn_down",
    )(a, w2, x1, gf, fnorm)


def _pad_lanes(v, n=LANES):
    return jnp.pad(v, (0, n - v.shape[0])).reshape(1, n)


def _layer(x, ctx, mod, mod_c, prm):
    bsz, seq, d = x.shape
    clen = ctx.shape[1]
    rows = seq // GRID_W

    sh_m, sc_m, g_m, sh_f, sc_f, g_f = [m.reshape(bsz, 1, d) for m in jnp.split(mod, N_MOD, axis=-1)]
    csh_m, csc_m = [jnp.broadcast_to(m.reshape(1, 1, d), (bsz, 1, d))
                    for m in jnp.split(mod_c, N_MOD, axis=-1)[:2]]

    w_in = prm["w_in"]
    o = 0
    parts = {}
    for name, size in (("z", SSD_WIDTH), ("xs", SSD_WIDTH), ("b", SSD_BC), ("c", SSD_BC),
                       ("dt", 2 * SSD_HEADS), ("xr", LRU_WIDTH), ("yr", LRU_WIDTH)):
        parts[name] = w_in[:, o:o + size]
        o += size
    w_cat = jnp.concatenate([parts[k].astype(BF16) for k in ("xs", "xr", "z", "yr")]
                            + [prm["w_gate"].astype(BF16), parts["b"].astype(BF16), parts["c"].astype(BF16)], axis=1)
    b_cat = jnp.concatenate([jnp.zeros((COL_GS,), F32), prm["b_gate"],
                             jnp.zeros((NP - COL_BC,), F32)]).reshape(1, NP)
    w_dt = jnp.zeros((d, DT_COLS), F32)
    w_dt = w_dt.at[:, 0:SSD_HEADS].set(parts["dt"][:, :SSD_HEADS])
    w_dt = w_dt.at[:, LANES:LANES + SSD_HEADS].set(parts["dt"][:, SSD_HEADS:]).astype(BF16)
    g_mix = prm["norm_mix"].reshape(1, d)

    p_ctx, dt_ctx = _in_proj(ctx, csh_m, csc_m, g_mix, w_cat, b_cat, w_dt, tm=clen, ctx_subset=True)
    p_lat, dt_lat = _in_proj(x, sh_m, sc_m, g_mix, w_cat, b_cat, w_dt, *TILES["in_proj"], ctx_subset=False)

    cwx = prm["ssd_conv_w"][:, :SSD_WIDTH]
    cwbc = prm["ssd_conv_w"][:, SSD_WIDTH:]
    cbx = prm["ssd_conv_b"][:SSD_WIDTH].reshape(1, -1)
    cbbc = prm["ssd_conv_b"][SSD_WIDTH:].reshape(1, -1)
    dtb = [_pad_lanes(prm["ssd_dt_bias"][k]) for k in range(2)]
    alog = [_pad_lanes(prm["ssd_a_log"][k]) for k in range(2)]
    conv = (cwx, cbx, cwbc, cbbc)
    zero_state = jnp.zeros((bsz, SSD_STATE, SSD_WIDTH), F32)
    _, s_f, xc_ctx, bc_ctx = _ssd_sweep(p_ctx, dt_ctx, zero_state, dtb[0], alog[0], rev=False,
                                        col_bc=CTX_COL_BC, conv=conv)
    _, s_b = _ssd_sweep((xc_ctx, bc_ctx), dt_ctx, zero_state, dtb[1], alog[1], rev=True)
    y_f, _, xc_lat, bc_lat = _ssd_sweep(p_lat, dt_lat, s_f, dtb[0], alog[0], rev=False, col_bc=COL_BC, conv=conv)
    dskip = jnp.repeat(prm["ssd_d"], SSD_HEAD_DIM).reshape(1, SSD_WIDTH)
    yn, _ = _ssd_sweep((xc_lat, bc_lat), dt_lat, s_b, dtb[1], alog[1], rev=True,
                       extra=(y_f, p_lat, dskip, prm["ssd_norm"].reshape(1, SSD_WIDTH)))

    hd = LRU_HEAD_DIM
    wg = (jnp.concatenate([prm["lru_w_a"][0], prm["lru_w_x"][0],
                           prm["lru_w_a"][1], prm["lru_w_x"][1]], axis=-1) * 0.5).astype(BF16)
    bg = jnp.stack([prm["lru_b_a"][0].reshape(LRU_HEADS, hd), prm["lru_b_x"][0].reshape(LRU_HEADS, hd),
                    prm["lru_b_a"][1].reshape(LRU_HEADS, hd), prm["lru_b_x"][1].reshape(LRU_HEADS, hd)],
                   axis=1).reshape(LRU_HEADS, 1, 4 * hd) * 0.5
    lcw = prm["lru_conv_w"]
    lcb = prm["lru_conv_b"].reshape(1, -1)
    lam = prm["lru_lambda"]
    cw_ = 8
    cr_ = clen // cw_
    xr_ctx = p_ctx[:, :, COL_XR:COL_XR + LRU_WIDTH]
    xr_ctx = xr_ctx.reshape(bsz, cw_, cr_, LRU_WIDTH).transpose(0, 2, 1, 3).reshape(bsz, clen, LRU_WIDTH)
    zero_h = jnp.zeros((bsz, 2, LRU_WIDTH), F32)
    (f_ctx,) = _lru(xr_ctx, 0, None, 0, lcw, lcb, wg, bg, lam, zero_h, rows=cr_, width=cw_, emit=False,
                    hps=LRU_CTX_HEADS_PER_STEP)
    rg, _ = _lru(p_lat, COL_XR, p_lat, COL_YR, lcw, lcb, wg, bg, lam, f_ctx,
                 rows=rows, width=GRID_W, emit=True)

    m = _merge_a(yn, rg, prm["w_out_ssd"].astype(BF16), prm["w_out_lru"].astype(BF16), p_lat, *TILES["merge_a"])
    x1, h_ffn = _merge_b(m, prm["w_o"].astype(BF16), x, g_m, sh_f, sc_f, prm["norm_ffn"].reshape(1, d),
                         *TILES["merge_b"])

    act = _ffn_up(h_ffn, prm["ffn_w13"].astype(BF16), *TILES["ffn_up"])
    return _ffn_down(act, prm["ffn_w2"].astype(BF16), x1, g_f, prm["final_norm"].reshape(1, d), *TILES["ffn_down"])


def kernel(x, c, ctx, c_ctx, w_ada, b_ada, norm_mix, norm_ffn, w_in, ssd_conv_w, ssd_conv_b, ssd_dt_bias,
           ssd_a_log, ssd_d, ssd_norm, w_out_ssd, lru_conv_w, lru_conv_b, lru_w_a, lru_b_a, lru_w_x, lru_b_x,
           lru_lambda, w_out_lru, w_gate, b_gate, w_o, ffn_w13, ffn_w2, final_norm):
    assert w_ada.shape[0] == 1, "single-layer trunk"
    bsz = x.shape[0]
    cvec = jnp.zeros((8, D_MODEL), F32).at[:bsz].set(c).at[bsz].set(c_ctx)
    mod_all = _ada(cvec, w_ada[0], b_ada[0].reshape(1, -1))
    prm = dict(w_in=w_in[0], ssd_conv_w=ssd_conv_w[0], ssd_conv_b=ssd_conv_b[0], ssd_dt_bias=ssd_dt_bias[0],
               ssd_a_log=ssd_a_log[0], ssd_d=ssd_d[0], ssd_norm=ssd_norm[0], w_out_ssd=w_out_ssd[0],
               lru_conv_w=lru_conv_w[0], lru_conv_b=lru_conv_b[0], lru_w_a=lru_w_a[0], lru_b_a=lru_b_a[0],
               lru_w_x=lru_w_x[0], lru_b_x=lru_b_x[0], lru_lambda=lru_lambda[0], w_out_lru=w_out_lru[0],
               w_gate=w_gate[0], b_gate=b_gate[0], w_o=w_o[0], norm_mix=norm_mix[0], norm_ffn=norm_ffn[0],
               ffn_w13=ffn_w13[0], ffn_w2=ffn_w2[0], final_norm=final_norm)
    return _layer(x, ctx, mod_all[:bsz], mod_all[bsz], prm)
```

```python
import functools

import jax
import numpy as np
import jax.numpy as jnp
from jax import lax
from jax.experimental import pallas as pl
from jax.experimental.pallas import tpu as pltpu

F32 = jnp.float32
BF16 = jnp.bfloat16

D_MODEL = 2048
GRID_W = 64
N_MOD = 6
SSD_HEAD_DIM = 64
SSD_HEADS = 32
SSD_WIDTH = SSD_HEADS * SSD_HEAD_DIM
SSD_GROUPS = 4
SSD_HPG = SSD_HEADS // SSD_GROUPS
SSD_STATE = 128
SSD_BC = SSD_GROUPS * SSD_STATE
SSD_CHUNK = 128
CONV_W = 4
CONV_PAD_LEFT = 2
LRU_WIDTH = D_MODEL
LRU_HEADS = 16
LRU_HEAD_DIM = LRU_WIDTH // LRU_HEADS
LRU_C = 8.0
EPS = 1e-6

LANES = 128
MXU_COLS = 256
LRU_CTX_HEADS_PER_STEP = 8
HALO = 16
VMEM_LIMIT = 56 * 1024 * 1024

COL_XS = 0
COL_XR = 2048
COL_Z = 4096
COL_YR = 6144
COL_GS = 8192
COL_GR = 10240
COL_BC = 12288
NP = 13312
CTX_COL_BC = 4096
CTX_NP = 5120
DT_COLS = 256


TILES = {"in_proj": (1024,), "merge_a": (512,), "merge_b": (512,), "ffn_up": (2048, 512), "ffn_down": (512,)}


def _params(sem, vmem=VMEM_LIMIT):
    return pltpu.CompilerParams(dimension_semantics=sem, vmem_limit_bytes=vmem)


LOG2E = 1.4426950408889634


def _sigmoid(x):
    return 1.0 / (1.0 + jnp.exp2(x * (-LOG2E)))


def _softplus(x):
    return jnp.maximum(x, 0.0) + jnp.log1p(jnp.exp(-jnp.abs(x)))


def _gelu_tanh(x):
    c1 = 0.7978845608028654
    half_x = 0.5 * x
    return half_x * jnp.tanh(x * (c1 + (c1 * 0.044715) * (x * x))) + half_x


def _norm_mod(x, g, shift, scale):
    xf = x.astype(F32)
    ms = jnp.mean(xf * xf, axis=-1, keepdims=True)
    return (xf * lax.rsqrt(ms + EPS)) * (g * (1.0 + scale)) + shift


def _ada_kernel(c_ref, w_ref, b_ref, o_ref):
    c = c_ref[...]
    s = (c * _sigmoid(c)).astype(BF16)
    o_ref[...] = jnp.dot(s, w_ref[...].astype(BF16), preferred_element_type=F32) + b_ref[...]


def _ada(cvec, w, b):
    m, d = cvec.shape
    n = w.shape[1]
    tn = 1024
    return pl.pallas_call(
        _ada_kernel,
        out_shape=jax.ShapeDtypeStruct((m, n), F32),
        grid=(n // tn,),
        in_specs=[pl.BlockSpec((m, d), lambda j: (0, 0)),
                  pl.BlockSpec((d, tn), lambda j: (0, j)),
                  pl.BlockSpec((1, tn), lambda j: (0, j))],
        out_specs=pl.BlockSpec((m, tn), lambda j: (0, j)),
        compiler_params=_params(("arbitrary",)),
        name="ada",
    )(cvec, w, b)


def _in_proj_kernel(x_ref, sh_ref, sc_ref, g_ref, w_ref, b_ref, wdt_ref, p_ref, dt_ref, hn_ref,
                    *, gate_lo, gate_hi):
    n = pl.program_id(2)

    @pl.when(n == 0)
    def _():
        h = _norm_mod(x_ref[...], g_ref[...], sh_ref[...], sc_ref[...]).astype(BF16)
        hn_ref[...] = h
        dt_ref[...] = jnp.dot(h, wdt_ref[...], preferred_element_type=F32)

    def project(gate):
        hn = hn_ref[...]
        for j in range(0, p_ref.shape[1], MXU_COLS):
            acc = jnp.dot(hn, w_ref[:, j:j + MXU_COLS], preferred_element_type=F32) + b_ref[:, j:j + MXU_COLS]
            p_ref[:, j:j + MXU_COLS] = (_sigmoid(acc) if gate else acc).astype(BF16)

    if gate_lo == gate_hi:
        project(False)
        return
    is_gate = jnp.logical_and(n >= gate_lo, n < gate_hi)
    pl.when(is_gate)(functools.partial(project, True))
    pl.when(jnp.logical_not(is_gate))(functools.partial(project, False))


def _in_proj(x, shift, scale, g, w_cat, b_cat, w_dt, tm, ctx_subset):
    bsz, s, d = x.shape
    tn = 1024
    if ctx_subset:
        n_out = CTX_NP
        gates = (0, 0)
        wtile = lambda n: jnp.where(n < CTX_COL_BC // tn, n, COL_BC // tn)
    else:
        n_out = NP
        gates = (COL_GS // tn, COL_BC // tn)
        wtile = lambda n: n
    kern = functools.partial(_in_proj_kernel, gate_lo=gates[0], gate_hi=gates[1])
    return pl.pallas_call(
        kern,
        out_shape=(jax.ShapeDtypeStruct((bsz, s, n_out), BF16),
                   jax.ShapeDtypeStruct((bsz, s, DT_COLS), F32)),
        grid=(bsz, s // tm, n_out // tn),
        in_specs=[pl.BlockSpec((None, tm, d), lambda b, i, n: (b, i, 0)),
                  pl.BlockSpec((None, 1, d), lambda b, i, n: (b, 0, 0)),
                  pl.BlockSpec((None, 1, d), lambda b, i, n: (b, 0, 0)),
                  pl.BlockSpec((1, d), lambda b, i, n: (0, 0)),
                  pl.BlockSpec((d, tn), lambda b, i, n: (0, wtile(n))),
                  pl.BlockSpec((1, tn), lambda b, i, n: (0, wtile(n))),
                  pl.BlockSpec((d, DT_COLS), lambda b, i, n: (0, 0))],
        out_specs=(pl.BlockSpec((None, tm, tn), lambda b, i, n: (b, i, n)),
                   pl.BlockSpec((None, tm, DT_COLS), lambda b, i, n: (b, i, 0))),
        scratch_shapes=[pltpu.VMEM((tm, d), BF16)],
        compiler_params=_params(("arbitrary", "arbitrary", "arbitrary")),
        name="in_proj",
    )(x, shift, scale, g, w_cat, b_cat, w_dt)


def _split3(v):
    v1 = v.astype(BF16)
    r1 = v - v1.astype(F32)
    v2 = r1.astype(BF16)
    r2 = r1 - v2.astype(F32)
    return v1, v2, r2.astype(BF16)


def _ssd_kernel(*refs, rev, final, nc):
    it = iter(refs)
    if rev:
        xsc_ref, bcc_ref = next(it), next(it)
    else:
        xs_ref, xsp_ref, xsn_ref, bc_ref, bcp_ref, bcn_ref = [next(it) for _ in range(6)]
        cwx_ref, cbx_ref, cwbc_ref, cbbc_ref, shc_ref, sht_ref, shb_ref = [next(it) for _ in range(7)]
    dt_ref, dtb_ref, alog_ref, tri_ref, s0_ref = [next(it) for _ in range(5)]
    if final:
        yf_ref, z_ref, dskip_ref, nw_ref = [next(it) for _ in range(4)]
    y_ref, sfin_ref = next(it), next(it)
    if not rev:
        xsc_out_ref, bcc_out_ref = next(it), next(it)
    st_ref = next(it)
    if final:
        gat_ref = next(it)

    i = pl.program_id(1)
    c = (nc - 1 - i) if rev else i
    q = SSD_CHUNK

    @pl.when(i == 0)
    def _():
        st_ref[...] = s0_ref[...]

    tri = tri_ref[...]

    if rev:
        xs = xsc_ref[...].astype(F32)
        bcv = bcc_ref[...].astype(F32)
    else:
        has_prev = c > 0
        has_next = c < nc - 1
        shift_cur = shc_ref[...]
        shift_top = sht_ref[...]
        shift_bot = shb_ref[...]

        def conv_silu(cur_ref, prev_ref, next_ref, w_ref, b_ref):
            wb = [w_ref[k:k + 1, :].astype(BF16) for k in range(CONV_W)]
            cur = cur_ref[...]
            taps = jnp.concatenate([cur * wb[k] for k in range(CONV_W)], axis=0)
            out = jnp.dot(shift_cur, taps, preferred_element_type=F32) + b_ref[...]
            prev = jnp.where(has_prev, prev_ref[...], jnp.zeros_like(prev_ref))
            top = jnp.concatenate([prev * wb[0], prev * wb[1]], axis=0)
            nxt_rows = jnp.where(has_next, next_ref[...], jnp.zeros_like(next_ref))
            bot = nxt_rows * wb[CONV_W - 1]
            head = out[0:HALO] + jnp.dot(shift_top, top, preferred_element_type=F32)
            tail = out[q - HALO:q] + jnp.dot(shift_bot, bot, preferred_element_type=F32)
            out = jnp.concatenate([head, out[HALO:q - HALO], tail], axis=0)
            return out * _sigmoid(out)

        xs = conv_silu(xs_ref, xsp_ref, xsn_ref, cwx_ref, cbx_ref)
        bcv = conv_silu(bc_ref, bcp_ref, bcn_ref, cwbc_ref, cbbc_ref)
        xsc_out_ref[...] = xs.astype(BF16)
        bcc_out_ref[...] = bcv.astype(BF16)

    dt = _softplus(dt_ref[...] + dtb_ref[...])
    a = dt * (-jnp.exp(alog_ref[...]))

    ii = lax.broadcasted_iota(jnp.int32, (q, q), 0)
    jj = lax.broadcasted_iota(jnp.int32, (q, q), 1)
    mask = (jj >= ii) if rev else (jj <= ii)

    acum = sum(jnp.dot(tri, p, preferred_element_type=F32) for p in _split3(a))
    a_t = a.T
    dt_t = dt.T
    nt = (((1,), (1,)), ((), ()))
    acum_t = sum(lax.dot_general(p, tri, nt, preferred_element_type=F32) for p in _split3(a_t))

    edge = 0 if rev else q - 1
    total = acum[edge:edge + 1, :]
    cd = jnp.exp(total)
    w_t = dt_t * jnp.exp(acum_t[:, edge:edge + 1] - acum_t)
    acum2 = acum * LOG2E
    adj2_t = acum_t * LOG2E - jnp.log2(dt_t)

    lane = lax.broadcasted_iota(jnp.int32, (q, LANES), 1)
    lo_half = lane < SSD_HEAD_DIM
    lo_row = lo_half[0:1, :]

    gsq = jnp.zeros((q, LANES), F32)
    for g in range(SSD_GROUPS):
        b_g = bcv[:, g * SSD_STATE:(g + 1) * SSD_STATE]
        c_g = bcv[:, SSD_BC + g * SSD_STATE:SSD_BC + (g + 1) * SSD_STATE]
        b_gb = b_g.astype(BF16)
        c_gb = c_g.astype(BF16)
        cb = lax.dot_general(c_gb, b_gb, nt, preferred_element_type=F32)
        gw = SSD_HPG * SSD_HEAD_DIM
        s_in = st_ref[:, g * gw:(g + 1) * gw].astype(BF16)
        yoff = jnp.dot(c_gb, s_in, preferred_element_type=F32)
        b_t = b_g.T
        for kp in range(SSD_HPG // 2):
            pair = g * (SSD_HPG // 2) + kp
            col = pair * LANES
            ms, ecols, bws, cds = [], [], [], []
            for e in (2 * pair, 2 * pair + 1):
                colb = jnp.broadcast_to(acum2[:, e:e + 1], (q, q))
                seg = colb - adj2_t[e:e + 1, :]
                lmat_dt = jnp.exp2(jnp.where(mask, seg, -1e30))
                ms.append((cb * lmat_dt).astype(BF16))
                ecols.append(jnp.exp2(colb))
                bws.append((b_t * w_t[e:e + 1, :]).astype(BF16))
                cds.append(jnp.broadcast_to(cd[:, e:e + 1], (1, LANES)))
            xp = xs[:, col:col + LANES]
            rhs = jnp.concatenate([jnp.where(lo_half, xp, 0.0).astype(BF16),
                                   jnp.where(lo_half, 0.0, xp).astype(BF16)], axis=0)
            lhs = jnp.concatenate(ms, axis=1)
            ydiag = jnp.dot(lhs, rhs, preferred_element_type=F32)
            y_pair = ydiag + jnp.where(lo_half, ecols[0], ecols[1]) * yoff[:, kp * LANES:(kp + 1) * LANES]
            cd_pair = jnp.where(lo_row, cds[0], cds[1])
            ds = jnp.dot(jnp.concatenate(bws, axis=1), rhs, preferred_element_type=F32)
            st_ref[:, col:col + LANES] = st_ref[:, col:col + LANES] * cd_pair + ds
            if final:
                y_tot = y_pair + yf_ref[:, col:col + LANES] + dskip_ref[:, col:col + LANES] * xp
                zz = z_ref[:, col:col + LANES].astype(F32)
                gated = y_tot * (zz * _sigmoid(zz))
                gat_ref[:, col:col + LANES] = gated
                gsq = gsq + gated * gated
            else:
                y_ref[:, col:col + LANES] = y_pair

    if final:
        ms = jnp.sum(gsq, axis=-1, keepdims=True) * (1.0 / SSD_WIDTH)
        y_ref[...] = (gat_ref[...] * lax.rsqrt(ms + EPS) * nw_ref[...]).astype(BF16)

    @pl.when(i == nc - 1)
    def _():
        sfin_ref[...] = st_ref[...]


def _ssd_constants(rev):
    q = SSD_CHUNK
    t = np.arange(q)[:, None]
    r = np.arange(CONV_W * q)[None, :]
    shift_cur = (r % q) == t + r // q - CONV_PAD_LEFT
    shift_top = np.zeros((HALO, 2 * HALO), bool)
    shift_top[0, HALO - 2] = shift_top[1, HALO - 1] = shift_top[0, 2 * HALO - 1] = True
    shift_bot = np.zeros((HALO, HALO), bool)
    shift_bot[HALO - 1, 0] = True
    i = np.arange(q)[:, None]
    j = np.arange(q)[None, :]
    tri = (j >= i) if rev else (j <= i)
    return [jnp.asarray(m, BF16) for m in (shift_cur, shift_top, shift_bot, tri)]


def _ssd_sweep(src, dtraw, s0, dtb, alog, *, rev, col_bc=None, conv=None, extra=None):
    bsz, s, _ = dtraw.shape
    nc = s // SSD_CHUNK
    q = SSD_CHUNK
    hb = q // HALO
    final = extra is not None
    assert rev or not final

    def cidx(i):
        return (nc - 1 - i) if rev else i

    def cur(col_block):
        return lambda b, i: (b, cidx(i), col_block)

    def prev(col_block):
        return lambda b, i: (b, jnp.maximum(cidx(i) * hb - 1, 0), col_block)

    def nxt(col_block):
        return lambda b, i: (b, jnp.minimum(cidx(i) * hb + hb, s // HALO - 1), col_block)

    const2 = lambda b, i: (0, 0)
    xw_, bw_ = SSD_WIDTH, 2 * SSD_BC
    shift_cur, shift_top, shift_bot, tri = _ssd_constants(rev)
    if rev:
        in_specs = [pl.BlockSpec((None, q, xw_), cur(0)), pl.BlockSpec((None, q, bw_), cur(0))]
        args = list(src)
    else:
        p = src
        in_specs = [pl.BlockSpec((None, q, xw_), cur(COL_XS // xw_)),
                    pl.BlockSpec((None, HALO, xw_), prev(COL_XS // xw_)),
                    pl.BlockSpec((None, HALO, xw_), nxt(COL_XS // xw_)),
                    pl.BlockSpec((None, q, bw_), cur(col_bc // bw_)),
                    pl.BlockSpec((None, HALO, bw_), prev(col_bc // bw_)),
                    pl.BlockSpec((None, HALO, bw_), nxt(col_bc // bw_)),
                    pl.BlockSpec((CONV_W, xw_), const2),
                    pl.BlockSpec((1, xw_), const2),
                    pl.BlockSpec((CONV_W, bw_), const2),
                    pl.BlockSpec((1, bw_), const2),
                    pl.BlockSpec((q, CONV_W * q), const2),
                    pl.BlockSpec((HALO, 2 * HALO), const2),
                    pl.BlockSpec((HALO, HALO), const2)]
        args = [p, p, p, p, p, p, *conv, shift_cur, shift_top, shift_bot]
    in_specs += [pl.BlockSpec((None, q, LANES), cur(1 if rev else 0)),
                 pl.BlockSpec((1, LANES), const2),
                 pl.BlockSpec((1, LANES), const2),
                 pl.BlockSpec((q, q), const2),
                 pl.BlockSpec((None, SSD_STATE, xw_), lambda b, i: (b, 0, 0))]
    args += [dtraw, dtb, alog, tri, s0]
    scratch = [pltpu.VMEM((SSD_STATE, xw_), F32)]
    if final:
        yf, z_src, dskip, nw = extra
        in_specs += [pl.BlockSpec((None, q, xw_), cur(0)),
                     pl.BlockSpec((None, q, xw_), cur(COL_Z // xw_)),
                     pl.BlockSpec((1, xw_), const2),
                     pl.BlockSpec((1, xw_), const2)]
        args += [yf, z_src, dskip, nw]
        scratch.append(pltpu.VMEM((q, xw_), F32))
    out_shape = [jax.ShapeDtypeStruct((bsz, s, xw_), BF16 if final else F32),
                 jax.ShapeDtypeStruct((bsz, SSD_STATE, xw_), F32)]
    out_specs = [pl.BlockSpec((None, q, xw_), cur(0)),
                 pl.BlockSpec((None, SSD_STATE, xw_), lambda b, i: (b, 0, 0))]
    if not rev:
        out_shape += [jax.ShapeDtypeStruct((bsz, s, xw_), BF16), jax.ShapeDtypeStruct((bsz, s, bw_), BF16)]
        out_specs += [pl.BlockSpec((None, q, xw_), cur(0)), pl.BlockSpec((None, q, bw_), cur(0))]
    kern = functools.partial(_ssd_kernel, rev=rev, final=final, nc=nc)
    return pl.pallas_call(
        kern,
        out_shape=tuple(out_shape),
        grid=(bsz, nc),
        in_specs=in_specs,
        out_specs=tuple(out_specs),
        scratch_shapes=scratch,
        compiler_params=_params(("arbitrary", "arbitrary")),
        name="ssd_" + ("bwd" if rev else "fwd") + ("_final" if final else ""),
    )(*args)


def _lru_kernel(*refs, rows, width, tb, emit, hps):
    n_in = 8 if emit else 7
    n_out = 2 if emit else 1
    per_head_leading = (4, 5) if emit else (3, 4)
    for hh in range(hps):
        lanes = slice(hh * LRU_HEAD_DIM, (hh + 1) * LRU_HEAD_DIM)
        views = [r.at[hh] if k in per_head_leading else r.at[:, lanes] for k, r in enumerate(refs[:n_in + n_out])]
        _lru_head(*views, *refs[n_in + n_out:], rows=rows, width=width, tb=tb, emit=emit)


def _lru_head(*refs, rows, width, tb, emit):
    if emit:
        (xr_ref, yr_ref, cw_ref, cb_ref, wg_ref, bg_ref, lam_ref, h0_ref,
         rg_ref, fin_ref,
         xe_ref, af_ref, uf_ref, ab_ref, ub_ref, sh_ref, fl_ref, pe_ref, cinf_ref, cinb_ref) = refs
    else:
        (xr_ref, cw_ref, cb_ref, wg_ref, bg_ref, lam_ref, h0_ref,
         fin_ref,
         xe_ref, af_ref, uf_ref, ab_ref, ub_ref, sh_ref, fl_ref, pe_ref, cinf_ref, cinb_ref) = refs
    R, W = rows, width
    hd = LRU_HEAD_DIM
    rb = tb // W

    xe_ref[2:2 + R] = xr_ref[...].astype(F32).reshape(R, W, hd)
    zero8 = jnp.zeros((8, hd), F32)
    sh_ref[0:8, :] = zero8
    sh_ref[8 + W:16 + W, :] = zero8
    for src, dst, off in ((R - 2, 0, 7), (R - 1, 1, 7), (0, R + 2, 9)):
        sh_ref[8:8 + W, :] = xe_ref[2 + src]
        xe_ref[dst] = sh_ref[off:off + W, :]

    a_scale = _softplus(-lam_ref[...]) * (-0.5 * LRU_C * LOG2E)
    cwv = cw_ref[...]
    cbv = cb_ref[...]
    wg = wg_ref[...]
    bg = bg_ref[...]

    def gate_body(blk, carry):
        r0 = blk * rb
        xc = cbv
        for k in range(CONV_W):
            xc = xc + cwv[k:k + 1, :] * xe_ref[pl.ds(r0 + k, rb)]
        xc2 = xc.reshape(rb * W, hd)
        pre = jnp.dot(xc2.astype(BF16), wg, preferred_element_type=F32) + bg
        half_x = 0.5 * xc2
        for d, (a_ref, u_ref) in enumerate(((af_ref, uf_ref), (ab_ref, ub_ref))):
            t_r = jnp.tanh(pre[:, (2 * d) * hd:(2 * d + 1) * hd])
            t_i = jnp.tanh(pre[:, (2 * d + 1) * hd:(2 * d + 2) * hd])
            k = a_scale[d:d + 1, :]
            a = jnp.exp2(t_r * k + k)
            v = 1.0 - a * a
            u = (v * lax.rsqrt(jnp.maximum(v, 1e-37))) * (t_i * half_x + half_x)
            a_ref[pl.ds(r0, rb)] = a.reshape(rb, W, hd)
            u_ref[pl.ds(r0, rb)] = u.reshape(rb, W, hd)
        return carry

    lax.fori_loop(0, R // rb, gate_body, 0)

    zeros = jnp.zeros((W, hd), F32)
    ones = jnp.ones((W, hd), F32)

    def scan_step(a_ref, u_ref, r, h, p):
        a = a_ref[r]
        h = a * h + u_ref[r]
        p = a * p
        u_ref[r] = h
        a_ref[r] = p
        return h, p

    def scan_body(t, carry):
        hf, pf, hb, pb = carry
        hf, pf = scan_step(af_ref, uf_ref, t, hf, pf)
        hb, pb = scan_step(ab_ref, ub_ref, R - 1 - t, hb, pb)
        return hf, pf, hb, pb

    def carry_chain(h_end, p_end, h0, cin_ref, reverse):
        fl_ref[...] = h_end
        pe_ref[...] = p_end

        def body(t, carry):
            cc = (W - 1 - t) if reverse else t
            cin_ref[pl.ds(cc, 1), :] = carry
            return fl_ref[pl.ds(cc, 1), :] + pe_ref[pl.ds(cc, 1), :] * carry
        return lax.fori_loop(0, W, body, h0)

    hf_end, pf_end, hb_end, pb_end = lax.fori_loop(0, R, scan_body, (zeros, ones, zeros, ones), unroll=4)
    fin_f = carry_chain(hf_end, pf_end, h0_ref[0:1, :], cinf_ref, False)
    fin_b = carry_chain(hb_end, pb_end, h0_ref[1:2, :], cinb_ref, True)
    fin_ref[0:1, :] = fin_f
    fin_ref[1:2, :] = fin_b

    if emit:
        cin_f = cinf_ref[...]
        cin_b = cinb_ref[...]

        def out_body(r, carry):
            h = (uf_ref[r] + af_ref[r] * cin_f) + (ub_ref[r] + ab_ref[r] * cin_b)
            row0 = pl.multiple_of(r * W, W)
            y = yr_ref[pl.ds(row0, W), :].astype(F32)
            rg_ref[pl.ds(row0, W), :] = (h * _gelu_tanh(y)).astype(BF16)
            return carry

        lax.fori_loop(0, R, out_body, 0, unroll=2)


def _lru(xr_src, xr_col, yr_src, yr_col, cw, cb, wg, bg, lam, h0, *, rows, width, emit, hps=1):
    bsz, s, _ = xr_src.shape
    hd = hps * LRU_HEAD_DIM
    tb = min(512, s)
    kern = functools.partial(_lru_kernel, rows=rows, width=width, tb=tb, emit=emit, hps=hps)
    head = lambda b, h: (0, h)
    in_specs = [pl.BlockSpec((None, s, hd), lambda b, h: (b, 0, xr_col // hd + h))]
    args = [xr_src]
    if emit:
        in_specs.append(pl.BlockSpec((None, s, hd), lambda b, h: (b, 0, yr_col // hd + h)))
        args.append(yr_src)
    in_specs += [pl.BlockSpec((CONV_W, hd), head),
                 pl.BlockSpec((1, hd), head),
                 pl.BlockSpec((hps, LRU_HEAD_DIM, 4 * LRU_HEAD_DIM), lambda b, h: (h, 0, 0)),
                 pl.BlockSpec((hps, 1, 4 * LRU_HEAD_DIM), lambda b, h: (h, 0, 0)),
                 pl.BlockSpec((2, hd), head),
                 pl.BlockSpec((None, 2, hd), lambda b, h: (b, 0, h))]
    args += [cw, cb, wg, bg, lam, h0]
    out_shape = [jax.ShapeDtypeStruct((bsz, 2, LRU_WIDTH), F32)]
    out_specs = [pl.BlockSpec((None, 2, hd), lambda b, h: (b, 0, h))]
    if emit:
        out_shape.insert(0, jax.ShapeDtypeStruct((bsz, s, LRU_WIDTH), BF16))
        out_specs.insert(0, pl.BlockSpec((None, s, hd), lambda b, h: (b, 0, h)))
    hl = LRU_HEAD_DIM
    big = pltpu.VMEM((rows, width, hl), F32)
    scratch = [pltpu.VMEM((rows + 3, width, hl), F32), big, big, big, big,
               pltpu.VMEM((width + 16, hl), F32),
               pltpu.VMEM((width, hl), F32), pltpu.VMEM((width, hl), F32),
               pltpu.VMEM((width, hl), F32), pltpu.VMEM((width, hl), F32)]
    return pl.pallas_call(
        kern,
        out_shape=tuple(out_shape),
        grid=(bsz, LRU_HEADS // hps),
        in_specs=in_specs,
        out_specs=tuple(out_specs),
        scratch_shapes=scratch,
        compiler_params=_params(("arbitrary", "arbitrary")),
        name="lru" + ("" if emit else "_ctx"),
    )(*args)


def _merge_a_kernel(yn_ref, rg_ref, ws_ref, wl_ref, gs_ref, gr_ref, m_ref):
    yn = yn_ref[...]
    rg = rg_ref[...]
    for j in range(0, m_ref.shape[1], MXU_COLS):
        cols = slice(j, j + MXU_COLS)
        o_s = jnp.dot(yn, ws_ref[:, cols], preferred_element_type=F32)
        o_r = jnp.dot(rg, wl_ref[:, cols], preferred_element_type=F32)
        m_ref[:, cols] = (gs_ref[:, cols].astype(F32) * o_s + gr_ref[:, cols].astype(F32) * o_r).astype(BF16)


def _merge_a(yn, rg, ws, wl, p, tm):
    bsz, s, d = yn.shape
    row = lambda b, i: (b, i, 0)
    whole = lambda b, i: (0, 0)
    return pl.pallas_call(
        _merge_a_kernel,
        out_shape=jax.ShapeDtypeStruct((bsz, s, d), BF16),
        grid=(bsz, s // tm),
        in_specs=[pl.BlockSpec((None, tm, d), row),
                  pl.BlockSpec((None, tm, d), row),
                  pl.BlockSpec((d, d), whole),
                  pl.BlockSpec((d, d), whole),
                  pl.BlockSpec((None, tm, d), lambda b, i: (b, i, COL_GS // d)),
                  pl.BlockSpec((None, tm, d), lambda b, i: (b, i, COL_GR // d))],
        out_specs=pl.BlockSpec((None, tm, d), row),
        compiler_params=_params(("arbitrary", "arbitrary")),
        name="merge_a",
    )(yn, rg, ws, wl, p, p)


def _merge_b_kernel(m_ref, w_ref, x_ref, gm_ref, sh_ref, sc_ref, g_ref, x1_ref, h_ref):
    half = m_ref.shape[0] // 2
    for r in (0, half):
        rows = slice(r, r + half)
        mix = jnp.dot(m_ref[rows, :], w_ref[...], preferred_element_type=F32)
        x1 = x_ref[rows, :] + gm_ref[...] * mix
        x1_ref[rows, :] = x1
        h_ref[rows, :] = _norm_mod(x1, g_ref[...], sh_ref[...], sc_ref[...]).astype(BF16)


def _merge_b(m, w_o, x, gm, shift, scale, g, tm):
    bsz, s, d = m.shape
    row = lambda b, i: (b, i, 0)
    vec = lambda b, i: (b, 0, 0)
    return pl.pallas_call(
        _merge_b_kernel,
        out_shape=(jax.ShapeDtypeStruct((bsz, s, d), F32), jax.ShapeDtypeStruct((bsz, s, d), BF16)),
        grid=(bsz, s // tm),
        in_specs=[pl.BlockSpec((None, tm, d), row),
                  pl.BlockSpec((d, d), lambda b, i: (0, 0)),
                  pl.BlockSpec((None, tm, d), row),
                  pl.BlockSpec((None, 1, d), vec),
                  pl.BlockSpec((None, 1, d), vec),
                  pl.BlockSpec((None, 1, d), vec),
                  pl.BlockSpec((1, d), lambda b, i: (0, 0))],
        out_specs=(pl.BlockSpec((None, tm, d), row), pl.BlockSpec((None, tm, d), row)),
        compiler_params=_params(("arbitrary", "arbitrary")),
        name="merge_b",
    )(m, w_o, x, gm, shift, scale, g)


def _ffn_up_kernel(h_ref, w1_ref, w3_ref, a_ref):
    h = h_ref[...]
    for j in range(0, a_ref.shape[1], MXU_COLS):
        cols = slice(j, j + MXU_COLS)
        gate = jnp.dot(h, w1_ref[:, cols], preferred_element_type=F32)
        up = jnp.dot(h, w3_ref[:, cols], preferred_element_type=F32)
        a_ref[:, cols] = (gate * _sigmoid(gate) * up).astype(BF16)


def _ffn_up(h, w13, tm, tn):
    bsz, s, d = h.shape
    hid = w13.shape[1] // 2
    nb = hid // tn
    return pl.pallas_call(
        _ffn_up_kernel,
        out_shape=jax.ShapeDtypeStruct((bsz, s, hid), BF16),
        grid=(bsz, s // tm, nb),
        in_specs=[pl.BlockSpec((None, tm, d), lambda b, i, n: (b, i, 0)),
                  pl.BlockSpec((d, tn), lambda b, i, n: (0, n)),
                  pl.BlockSpec((d, tn), lambda b, i, n: (0, nb + n))],
        out_specs=pl.BlockSpec((None, tm, tn), lambda b, i, n: (b, i, n)),
        compiler_params=_params(("arbitrary", "arbitrary", "arbitrary")),
        name="ffn_up",
    )(h, w13, w13)


def _ffn_down_kernel(a_ref, w_ref, x_ref, gf_ref, fn_ref, o_ref):
    x2 = x_ref[...] + gf_ref[...] * jnp.dot(a_ref[...], w_ref[...], preferred_element_type=F32)
    ms = jnp.mean(x2 * x2, axis=-1, keepdims=True)
    o_ref[...] = x2 * lax.rsqrt(ms + EPS) * fn_ref[...]


def _ffn_down(a, w2, x1, gf, fnorm, tm):
    bsz, s, hid = a.shape
    d = w2.shape[1]
    row = lambda b, i: (b, i, 0)
    return pl.pallas_call(
        _ffn_down_kernel,
        out_shape=jax.ShapeDtypeStruct((bsz, s, d), F32),
        grid=(bsz, s // tm),
        in_specs=[pl.BlockSpec((None, tm, hid), row),
                  pl.BlockSpec((hid, d), lambda b, i: (0, 0)),
                  pl.BlockSpec((None, tm, d), row),
                  pl.BlockSpec((None, 1, d), lambda b, i: (b, 0, 0)),
                  pl.BlockSpec((1, d), lambda b, i: (0, 0))],
        out_specs=pl.BlockSpec((None, tm, d), row),
        compiler_params=_params(("arbitrary", "arbitrary")),
        name="ffn_down",
    )(a, w2, x1, gf, fnorm)


def _pad_lanes(v, n=LANES):
    return jnp.pad(v, (0, n - v.shape[0])).reshape(1, n)


def _layer(x, ctx, mod, mod_c, prm):
    bsz, seq, d = x.shape
    clen = ctx.shape[1]
    rows = seq // GRID_W

    sh_m, sc_m, g_m, sh_f, sc_f, g_f = [m.reshape(bsz, 1, d) for m in jnp.split(mod, N_MOD, axis=-1)]
    csh_m, csc_m = [m.reshape(1, 1, d) for m in jnp.split(mod_c, N_MOD, axis=-1)[:2]]

    w_in = prm["w_in"]
    o = 0
    parts = {}
    for name, size in (("z", SSD_WIDTH), ("xs", SSD_WIDTH), ("b", SSD_BC), ("c", SSD_BC),
                       ("dt", 2 * SSD_HEADS), ("xr", LRU_WIDTH), ("yr", LRU_WIDTH)):
        parts[name] = w_in[:, o:o + size]
        o += size
    w_cat = jnp.concatenate([parts[k].astype(BF16) for k in ("xs", "xr", "z", "yr")]
                            + [prm["w_gate"].astype(BF16), parts["b"].astype(BF16), parts["c"].astype(BF16)], axis=1)
    b_cat = jnp.concatenate([jnp.zeros((COL_GS,), F32), prm["b_gate"],
                             jnp.zeros((NP - COL_BC,), F32)]).reshape(1, NP)
    w_dt = jnp.zeros((d, DT_COLS), F32)
    w_dt = w_dt.at[:, 0:SSD_HEADS].set(parts["dt"][:, :SSD_HEADS])
    w_dt = w_dt.at[:, LANES:LANES + SSD_HEADS].set(parts["dt"][:, SSD_HEADS:]).astype(BF16)
    g_mix = prm["norm_mix"].reshape(1, d)

    p_ctx, dt_ctx = _in_proj(ctx.reshape(1, bsz * clen, d), csh_m, csc_m, g_mix, w_cat, b_cat, w_dt,
                             tm=bsz * clen, ctx_subset=True)
    p_ctx = p_ctx.reshape(bsz, clen, CTX_NP)
    dt_ctx = dt_ctx.reshape(bsz, clen, DT_COLS)
    p_lat, dt_lat = _in_proj(x, sh_m, sc_m, g_mix, w_cat, b_cat, w_dt, *TILES["in_proj"], ctx_subset=False)

    cwx = prm["ssd_conv_w"][:, :SSD_WIDTH]
    cwbc = prm["ssd_conv_w"][:, SSD_WIDTH:]
    cbx = prm["ssd_conv_b"][:SSD_WIDTH].reshape(1, -1)
    cbbc = prm["ssd_conv_b"][SSD_WIDTH:].reshape(1, -1)
    dtb = [_pad_lanes(prm["ssd_dt_bias"][k]) for k in range(2)]
    alog = [_pad_lanes(prm["ssd_a_log"][k]) for k in range(2)]
    conv = (cwx, cbx, cwbc, cbbc)
    zero_state = jnp.zeros((bsz, SSD_STATE, SSD_WIDTH), F32)
    _, s_f, xc_ctx, bc_ctx = _ssd_sweep(p_ctx, dt_ctx, zero_state, dtb[0], alog[0], rev=False,
                                        col_bc=CTX_COL_BC, conv=conv)
    _, s_b = _ssd_sweep((xc_ctx, bc_ctx), dt_ctx, zero_state, dtb[1], alog[1], rev=True)
    y_f, _, xc_lat, bc_lat = _ssd_sweep(p_lat, dt_lat, s_f, dtb[0], alog[0], rev=False, col_bc=COL_BC, conv=conv)
    dskip = jnp.repeat(prm["ssd_d"], SSD_HEAD_DIM).reshape(1, SSD_WIDTH)
    yn, _ = _ssd_sweep((xc_lat, bc_lat), dt_lat, s_b, dtb[1], alog[1], rev=True,
                       extra=(y_f, p_lat, dskip, prm["ssd_norm"].reshape(1, SSD_WIDTH)))

    hd = LRU_HEAD_DIM
    wg = (jnp.concatenate([prm["lru_w_a"][0], prm["lru_w_x"][0],
                           prm["lru_w_a"][1], prm["lru_w_x"][1]], axis=-1) * 0.5).astype(BF16)
    bg = jnp.stack([prm["lru_b_a"][0].reshape(LRU_HEADS, hd), prm["lru_b_x"][0].reshape(LRU_HEADS, hd),
                    prm["lru_b_a"][1].reshape(LRU_HEADS, hd), prm["lru_b_x"][1].reshape(LRU_HEADS, hd)],
                   axis=1).reshape(LRU_HEADS, 1, 4 * hd) * 0.5
    lcw = prm["lru_conv_w"]
    lcb = prm["lru_conv_b"].reshape(1, -1)
    lam = prm["lru_lambda"]
    cw_ = 8
    cr_ = clen // cw_
    xr_ctx = p_ctx[:, :, COL_XR:COL_XR + LRU_WIDTH]
    xr_ctx = xr_ctx.reshape(bsz, cw_, cr_, LRU_WIDTH).transpose(0, 2, 1, 3).reshape(bsz, clen, LRU_WIDTH)
    zero_h = jnp.zeros((bsz, 2, LRU_WIDTH), F32)
    (f_ctx,) = _lru(xr_ctx, 0, None, 0, lcw, lcb, wg, bg, lam, zero_h, rows=cr_, width=cw_, emit=False,
                    hps=LRU_CTX_HEADS_PER_STEP)
    rg, _ = _lru(p_lat, COL_XR, p_lat, COL_YR, lcw, lcb, wg, bg, lam, f_ctx,
                 rows=rows, width=GRID_W, emit=True)

    m = _merge_a(yn, rg, prm["w_out_ssd"].astype(BF16), prm["w_out_lru"].astype(BF16), p_lat, *TILES["merge_a"])
    x1, h_ffn = _merge_b(m, prm["w_o"].astype(BF16), x, g_m, sh_f, sc_f, prm["norm_ffn"].reshape(1, d),
                         *TILES["merge_b"])

    act = _ffn_up(h_ffn, prm["ffn_w13"].astype(BF16), *TILES["ffn_up"])
    return _ffn_down(act, prm["ffn_w2"].astype(BF16), x1, g_f, prm["final_norm"].reshape(1, d), *TILES["ffn_down"])


def kernel(x, c, ctx, c_ctx, w_ada, b_ada, norm_mix, norm_ffn, w_in, ssd_conv_w, ssd_conv_b, ssd_dt_bias,
           ssd_a_log, ssd_d, ssd_norm, w_out_ssd, lru_conv_w, lru_conv_b, lru_w_a, lru_b_a, lru_w_x, lru_b_x,
           lru_lambda, w_out_lru, w_gate, b_gate, w_o, ffn_w13, ffn_w2, final_norm):
    assert w_ada.shape[0] == 1, "single-layer trunk"
    bsz = x.shape[0]
    cvec = jnp.zeros((8, D_MODEL), F32).at[:bsz].set(c).at[bsz].set(c_ctx)
    mod_all = _ada(cvec, w_ada[0], b_ada[0].reshape(1, -1))
    prm = dict(w_in=w_in[0], ssd_conv_w=ssd_conv_w[0], ssd_conv_b=ssd_conv_b[0], ssd_dt_bias=ssd_dt_bias[0],
               ssd_a_log=ssd_a_log[0], ssd_d=ssd_d[0], ssd_norm=ssd_norm[0], w_out_ssd=w_out_ssd[0],
               lru_conv_w=lru_conv_w[0], lru_conv_b=lru_conv_b[0], lru_w_a=lru_w_a[0], lru_b_a=lru_b_a[0],
               lru_w_x=lru_w_x[0], lru_b_x=lru_b_x[0], lru_lambda=lru_lambda[0], w_out_lru=w_out_lru[0],
               w_gate=w_gate[0], b_gate=b_gate[0], w_o=w_o[0], norm_mix=norm_mix[0], norm_ffn=norm_ffn[0],
               ffn_w13=ffn_w13[0], ffn_w2=ffn_w2[0], final_norm=final_norm)
    return _layer(x, ctx, mod_all[:bsz], mod_all[bsz], prm)
```

```python
import functools

import jax
import numpy as np
import jax.numpy as jnp
from jax import lax
from jax.experimental import pallas as pl
from jax.experimental.pallas import tpu as pltpu

F32 = jnp.float32
BF16 = jnp.bfloat16

D_MODEL = 2048
GRID_W = 64
N_MOD = 6
SSD_HEAD_DIM = 64
SSD_HEADS = 32
SSD_WIDTH = SSD_HEADS * SSD_HEAD_DIM
SSD_GROUPS = 4
SSD_HPG = SSD_HEADS // SSD_GROUPS
SSD_STATE = 128
SSD_BC = SSD_GROUPS * SSD_STATE
SSD_CHUNK = 128
CONV_W = 4
CONV_PAD_LEFT = 2
LRU_WIDTH = D_MODEL
LRU_HEADS = 16
LRU_HEAD_DIM = LRU_WIDTH // LRU_HEADS
LRU_C = 8.0
EPS = 1e-6

LANES = 128
MXU_COLS = 256
LRU_CTX_HEADS_PER_STEP = 8
HALO = 16
VMEM_LIMIT = 56 * 1024 * 1024

COL_Z = 0
COL_XS = 2048
COL_BC = 4096
COL_XR = 5120
COL_YR = 7168
COL_GS = 9216
COL_GR = 11264
NP = 13312
CTX_COL_XS = 0
CTX_COL_BC = 2048
CTX_COL_XR = 3072
CTX_NP = 5120
PROJ_TN = 1024
IN_DT_END = COL_XR + 2 * SSD_HEADS
DT_COLS = 256


TILES = {"in_proj": (1024,), "merge_a": (512,), "merge_b": (512,), "ffn_up": (2048, 512), "ffn_down": (512,)}


def _params(sem, vmem=VMEM_LIMIT):
    return pltpu.CompilerParams(dimension_semantics=sem, vmem_limit_bytes=vmem)


LOG2E = 1.4426950408889634


def _sigmoid(x):
    return 1.0 / (1.0 + jnp.exp2(x * (-LOG2E)))


def _softplus(x):
    return jnp.maximum(x, 0.0) + jnp.log1p(jnp.exp(-jnp.abs(x)))


def _gelu_tanh(x):
    c1 = 0.7978845608028654
    half_x = 0.5 * x
    return half_x * jnp.tanh(x * (c1 + (c1 * 0.044715) * (x * x))) + half_x


def _norm_mod(x, g, shift, scale):
    xf = x.astype(F32)
    ms = jnp.mean(xf * xf, axis=-1, keepdims=True)
    return (xf * lax.rsqrt(ms + EPS)) * (g * (1.0 + scale)) + shift


def _ada_kernel(c_ref, w_ref, b_ref, o_ref):
    c = c_ref[...]
    s = (c * _sigmoid(c)).astype(BF16)
    o_ref[...] = jnp.dot(s, w_ref[...].astype(BF16), preferred_element_type=F32) + b_ref[...]


def _ada(cvec, w, b):
    m, d = cvec.shape
    n = w.shape[1]
    tn = 1024
    return pl.pallas_call(
        _ada_kernel,
        out_shape=jax.ShapeDtypeStruct((m, n), F32),
        grid=(n // tn,),
        in_specs=[pl.BlockSpec((m, d), lambda j: (0, 0)),
                  pl.BlockSpec((d, tn), lambda j: (0, j)),
                  pl.BlockSpec((1, tn), lambda j: (0, j))],
        out_specs=pl.BlockSpec((m, tn), lambda j: (0, j)),
        compiler_params=_params(("arbitrary",)),
        name="ada",
    )(cvec, w, b)


def _in_proj_kernel(x_ref, sh_ref, sc_ref, g_ref, wa_ref, wxy_ref, wg_ref, b_ref, wdt_ref, p_ref, dt_ref, hn_ref,
                    *, n_a, n_xy, n_g):
    n = pl.program_id(2)

    @pl.when(n == 0)
    def _():
        h = _norm_mod(x_ref[...], g_ref[...], sh_ref[...], sc_ref[...]).astype(BF16)
        hn_ref[...] = h
        dt_ref[...] = jnp.dot(h, wdt_ref[...], preferred_element_type=F32)

    def project(w_ref, gate):
        hn = hn_ref[...]
        for j in range(0, p_ref.shape[1], MXU_COLS):
            acc = jnp.dot(hn, w_ref[:, j:j + MXU_COLS], preferred_element_type=F32) + b_ref[:, j:j + MXU_COLS]
            p_ref[:, j:j + MXU_COLS] = (_sigmoid(acc) if gate else acc).astype(BF16)

    from_a = n < n_a
    from_g = n >= n_a + n_xy
    pl.when(from_a)(functools.partial(project, wa_ref, False))
    pl.when(jnp.logical_and(jnp.logical_not(from_a), jnp.logical_not(from_g)))(
        functools.partial(project, wxy_ref, False))
    if n_g:
        pl.when(from_g)(functools.partial(project, wg_ref, True))


def _in_proj(x, shift, scale, g, w_a, w_xy, w_g, b_cat, w_dt, tm, ctx_subset):
    bsz, s, d = x.shape
    tn = PROJ_TN
    if ctx_subset:
        n_a, a_off, n_xy, n_g = 3, COL_XS // tn, LRU_WIDTH // tn, 0
        btile = lambda n: 0
    else:
        n_a, a_off, n_xy, n_g = COL_XR // tn, 0, (COL_GS - COL_XR) // tn, (NP - COL_GS) // tn
        btile = lambda n: n
    n_tiles = n_a + n_xy + n_g
    kern = functools.partial(_in_proj_kernel, n_a=n_a, n_xy=n_xy, n_g=n_g)
    return pl.pallas_call(
        kern,
        out_shape=(jax.ShapeDtypeStruct((bsz, s, n_tiles * tn), BF16),
                   jax.ShapeDtypeStruct((bsz, s, DT_COLS), F32)),
        grid=(bsz, s // tm, n_tiles),
        in_specs=[pl.BlockSpec((None, tm, d), lambda b, i, n: (b, i, 0)),
                  pl.BlockSpec((None, 1, d), lambda b, i, n: (b, 0, 0)),
                  pl.BlockSpec((None, 1, d), lambda b, i, n: (b, 0, 0)),
                  pl.BlockSpec((1, d), lambda b, i, n: (0, 0)),
                  pl.BlockSpec((d, tn), lambda b, i, n: (0, jnp.minimum(n, n_a - 1) + a_off)),
                  pl.BlockSpec((d, tn), lambda b, i, n: (0, jnp.clip(n - n_a, 0, n_xy - 1))),
                  pl.BlockSpec((d, tn), lambda b, i, n: (0, jnp.clip(n - n_a - n_xy, 0, max(n_g, 1) - 1))),
                  pl.BlockSpec((1, tn), lambda b, i, n: (0, btile(n))),
                  pl.BlockSpec((d, DT_COLS), lambda b, i, n: (0, 0))],
        out_specs=(pl.BlockSpec((None, tm, tn), lambda b, i, n: (b, i, n)),
                   pl.BlockSpec((None, tm, DT_COLS), lambda b, i, n: (b, i, 0))),
        scratch_shapes=[pltpu.VMEM((tm, d), BF16)],
        compiler_params=_params(("arbitrary", "arbitrary", "arbitrary")),
        name="in_proj",
    )(x, shift, scale, g, w_a, w_xy, w_g, b_cat, w_dt)


def _split3(v):
    v1 = v.astype(BF16)
    r1 = v - v1.astype(F32)
    v2 = r1.astype(BF16)
    r2 = r1 - v2.astype(F32)
    return v1, v2, r2.astype(BF16)


def _ssd_kernel(*refs, rev, final, nc):
    it = iter(refs)
    if rev:
        xsc_ref, bcc_ref = next(it), next(it)
    else:
        xs_ref, xsp_ref, xsn_ref, bc_ref, bcp_ref, bcn_ref = [next(it) for _ in range(6)]
        cwx_ref, cbx_ref, cwbc_ref, cbbc_ref, shc_ref, sht_ref, shb_ref = [next(it) for _ in range(7)]
    dt_ref, dtb_ref, alog_ref, tri_ref, s0_ref = [next(it) for _ in range(5)]
    if final:
        yf_ref, z_ref, dskip_ref, nw_ref = [next(it) for _ in range(4)]
    y_ref, sfin_ref = next(it), next(it)
    if not rev:
        xsc_out_ref, bcc_out_ref = next(it), next(it)
    st_ref = next(it)
    if final:
        gat_ref = next(it)

    i = pl.program_id(1)
    c = (nc - 1 - i) if rev else i
    q = SSD_CHUNK

    @pl.when(i == 0)
    def _():
        st_ref[...] = s0_ref[...]

    tri = tri_ref[...]

    if rev:
        xs = xsc_ref[...].astype(F32)
        bcv = bcc_ref[...].astype(F32)
    else:
        has_prev = c > 0
        has_next = c < nc - 1
        shift_cur = shc_ref[...]
        shift_top = sht_ref[...]
        shift_bot = shb_ref[...]

        def conv_silu(cur_ref, prev_ref, next_ref, w_ref, b_ref):
            wb = [w_ref[k:k + 1, :].astype(BF16) for k in range(CONV_W)]
            cur = cur_ref[...]
            taps = jnp.concatenate([cur * wb[k] for k in range(CONV_W)], axis=0)
            out = jnp.dot(shift_cur, taps, preferred_element_type=F32) + b_ref[...]
            prev = jnp.where(has_prev, prev_ref[...], jnp.zeros_like(prev_ref))
            top = jnp.concatenate([prev * wb[0], prev * wb[1]], axis=0)
            nxt_rows = jnp.where(has_next, next_ref[...], jnp.zeros_like(next_ref))
            bot = nxt_rows * wb[CONV_W - 1]
            head = out[0:HALO] + jnp.dot(shift_top, top, preferred_element_type=F32)
            tail = out[q - HALO:q] + jnp.dot(shift_bot, bot, preferred_element_type=F32)
            out = jnp.concatenate([head, out[HALO:q - HALO], tail], axis=0)
            return out * _sigmoid(out)

        xs = conv_silu(xs_ref, xsp_ref, xsn_ref, cwx_ref, cbx_ref)
        bcv = conv_silu(bc_ref, bcp_ref, bcn_ref, cwbc_ref, cbbc_ref)
        xsc_out_ref[...] = xs.astype(BF16)
        bcc_out_ref[...] = bcv.astype(BF16)

    dt = _softplus(dt_ref[...] + dtb_ref[...])
    a = dt * (-jnp.exp(alog_ref[...]))

    ii = lax.broadcasted_iota(jnp.int32, (q, q), 0)
    jj = lax.broadcasted_iota(jnp.int32, (q, q), 1)
    mask = (jj >= ii) if rev else (jj <= ii)

    acum = sum(jnp.dot(tri, p, preferred_element_type=F32) for p in _split3(a))
    a_t = a.T
    dt_t = dt.T
    nt = (((1,), (1,)), ((), ()))
    acum_t = sum(lax.dot_general(p, tri, nt, preferred_element_type=F32) for p in _split3(a_t))

    edge = 0 if rev else q - 1
    total = acum[edge:edge + 1, :]
    cd = jnp.exp(total)
    w_t = dt_t * jnp.exp(acum_t[:, edge:edge + 1] - acum_t)
    acum2 = acum * LOG2E
    adj2_t = acum_t * LOG2E - jnp.log2(dt_t)

    lane = lax.broadcasted_iota(jnp.int32, (q, LANES), 1)
    lo_half = lane < SSD_HEAD_DIM
    lo_row = lo_half[0:1, :]

    gsq = jnp.zeros((q, LANES), F32)
    for g in range(SSD_GROUPS):
        b_g = bcv[:, g * SSD_STATE:(g + 1) * SSD_STATE]
        c_g = bcv[:, SSD_BC + g * SSD_STATE:SSD_BC + (g + 1) * SSD_STATE]
        b_gb = b_g.astype(BF16)
        c_gb = c_g.astype(BF16)
        cb = lax.dot_general(c_gb, b_gb, nt, preferred_element_type=F32)
        gw = SSD_HPG * SSD_HEAD_DIM
        s_in = st_ref[:, g * gw:(g + 1) * gw].astype(BF16)
        yoff = jnp.dot(c_gb, s_in, preferred_element_type=F32)
        b_t = b_g.T
        for kp in range(SSD_HPG // 2):
            pair = g * (SSD_HPG // 2) + kp
            col = pair * LANES
            ms, ecols, bws, cds = [], [], [], []
            for e in (2 * pair, 2 * pair + 1):
                colb = jnp.broadcast_to(acum2[:, e:e + 1], (q, q))
                seg = colb - adj2_t[e:e + 1, :]
                lmat_dt = jnp.exp2(jnp.where(mask, seg, -1e30))
                ms.append((cb * lmat_dt).astype(BF16))
                ecols.append(jnp.exp2(colb))
                bws.append((b_t * w_t[e:e + 1, :]).astype(BF16))
                cds.append(jnp.broadcast_to(cd[:, e:e + 1], (1, LANES)))
            xp = xs[:, col:col + LANES]
            rhs = jnp.concatenate([jnp.where(lo_half, xp, 0.0).astype(BF16),
                                   jnp.where(lo_half, 0.0, xp).astype(BF16)], axis=0)
            lhs = jnp.concatenate(ms, axis=1)
            ydiag = jnp.dot(lhs, rhs, preferred_element_type=F32)
            y_pair = ydiag + jnp.where(lo_half, ecols[0], ecols[1]) * yoff[:, kp * LANES:(kp + 1) * LANES]
            cd_pair = jnp.where(lo_row, cds[0], cds[1])
            ds = jnp.dot(jnp.concatenate(bws, axis=1), rhs, preferred_element_type=F32)
            st_ref[:, col:col + LANES] = st_ref[:, col:col + LANES] * cd_pair + ds
            if final:
                y_tot = y_pair + yf_ref[:, col:col + LANES] + dskip_ref[:, col:col + LANES] * xp
                zz = z_ref[:, col:col + LANES].astype(F32)
                gated = y_tot * (zz * _sigmoid(zz))
                gat_ref[:, col:col + LANES] = gated
                gsq = gsq + gated * gated
            else:
                y_ref[:, col:col + LANES] = y_pair

    if final:
        ms = jnp.sum(gsq, axis=-1, keepdims=True) * (1.0 / SSD_WIDTH)
        y_ref[...] = (gat_ref[...] * lax.rsqrt(ms + EPS) * nw_ref[...]).astype(BF16)

    @pl.when(i == nc - 1)
    def _():
        sfin_ref[...] = st_ref[...]


def _ssd_constants(rev):
    q = SSD_CHUNK
    t = np.arange(q)[:, None]
    r = np.arange(CONV_W * q)[None, :]
    shift_cur = (r % q) == t + r // q - CONV_PAD_LEFT
    shift_top = np.zeros((HALO, 2 * HALO), bool)
    shift_top[0, HALO - 2] = shift_top[1, HALO - 1] = shift_top[0, 2 * HALO - 1] = True
    shift_bot = np.zeros((HALO, HALO), bool)
    shift_bot[HALO - 1, 0] = True
    i = np.arange(q)[:, None]
    j = np.arange(q)[None, :]
    tri = (j >= i) if rev else (j <= i)
    return [jnp.asarray(m, BF16) for m in (shift_cur, shift_top, shift_bot, tri)]


def _ssd_sweep(src, dtraw, s0, dtb, alog, *, rev, col_xs=None, col_bc=None, conv=None, extra=None):
    bsz, s, _ = dtraw.shape
    nc = s // SSD_CHUNK
    q = SSD_CHUNK
    hb = q // HALO
    final = extra is not None
    assert rev or not final

    def cidx(i):
        return (nc - 1 - i) if rev else i

    def cur(col_block):
        return lambda b, i: (b, cidx(i), col_block)

    def prev(col_block):
        return lambda b, i: (b, jnp.maximum(cidx(i) * hb - 1, 0), col_block)

    def nxt(col_block):
        return lambda b, i: (b, jnp.minimum(cidx(i) * hb + hb, s // HALO - 1), col_block)

    const2 = lambda b, i: (0, 0)
    xw_, bw_ = SSD_WIDTH, 2 * SSD_BC
    shift_cur, shift_top, shift_bot, tri = _ssd_constants(rev)
    if rev:
        in_specs = [pl.BlockSpec((None, q, xw_), cur(0)), pl.BlockSpec((None, q, bw_), cur(0))]
        args = list(src)
    else:
        p = src
        in_specs = [pl.BlockSpec((None, q, xw_), cur(col_xs // xw_)),
                    pl.BlockSpec((None, HALO, xw_), prev(col_xs // xw_)),
                    pl.BlockSpec((None, HALO, xw_), nxt(col_xs // xw_)),
                    pl.BlockSpec((None, q, bw_), cur(col_bc // bw_)),
                    pl.BlockSpec((None, HALO, bw_), prev(col_bc // bw_)),
                    pl.BlockSpec((None, HALO, bw_), nxt(col_bc // bw_)),
                    pl.BlockSpec((CONV_W, xw_), const2),
                    pl.BlockSpec((1, xw_), const2),
                    pl.BlockSpec((CONV_W, bw_), const2),
                    pl.BlockSpec((1, bw_), const2),
                    pl.BlockSpec((q, CONV_W * q), const2),
                    pl.BlockSpec((HALO, 2 * HALO), const2),
                    pl.BlockSpec((HALO, HALO), const2)]
        args = [p, p, p, p, p, p, *conv, shift_cur, shift_top, shift_bot]
    in_specs += [pl.BlockSpec((None, q, LANES), cur(1 if rev else 0)),
                 pl.BlockSpec((1, LANES), const2),
                 pl.BlockSpec((1, LANES), const2),
                 pl.BlockSpec((q, q), const2),
                 pl.BlockSpec((None, SSD_STATE, xw_), lambda b, i: (b, 0, 0))]
    args += [dtraw, dtb, alog, tri, s0]
    scratch = [pltpu.VMEM((SSD_STATE, xw_), F32)]
    if final:
        yf, z_src, dskip, nw = extra
        in_specs += [pl.BlockSpec((None, q, xw_), cur(0)),
                     pl.BlockSpec((None, q, xw_), cur(COL_Z // xw_)),
                     pl.BlockSpec((1, xw_), const2),
                     pl.BlockSpec((1, xw_), const2)]
        args += [yf, z_src, dskip, nw]
        scratch.append(pltpu.VMEM((q, xw_), F32))
    out_shape = [jax.ShapeDtypeStruct((bsz, s, xw_), BF16 if final else F32),
                 jax.ShapeDtypeStruct((bsz, SSD_STATE, xw_), F32)]
    out_specs = [pl.BlockSpec((None, q, xw_), cur(0)),
                 pl.BlockSpec((None, SSD_STATE, xw_), lambda b, i: (b, 0, 0))]
    if not rev:
        out_shape += [jax.ShapeDtypeStruct((bsz, s, xw_), BF16), jax.ShapeDtypeStruct((bsz, s, bw_), BF16)]
        out_specs += [pl.BlockSpec((None, q, xw_), cur(0)), pl.BlockSpec((None, q, bw_), cur(0))]
    kern = functools.partial(_ssd_kernel, rev=rev, final=final, nc=nc)
    return pl.pallas_call(
        kern,
        out_shape=tuple(out_shape),
        grid=(bsz, nc),
        in_specs=in_specs,
        out_specs=tuple(out_specs),
        scratch_shapes=scratch,
        compiler_params=_params(("arbitrary", "arbitrary")),
        name="ssd_" + ("bwd" if rev else "fwd") + ("_final" if final else ""),
    )(*args)


def _lru_kernel(*refs, rows, width, tb, emit, hps):
    n_in = 8 if emit else 7
    n_out = 2 if emit else 1
    per_head_leading = (4, 5) if emit else (3, 4)
    for hh in range(hps):
        lanes = slice(hh * LRU_HEAD_DIM, (hh + 1) * LRU_HEAD_DIM)
        views = [r.at[hh] if k in per_head_leading else r.at[:, lanes] for k, r in enumerate(refs[:n_in + n_out])]
        _lru_head(*views, *refs[n_in + n_out:], rows=rows, width=width, tb=tb, emit=emit)


def _lru_head(*refs, rows, width, tb, emit):
    if emit:
        (xr_ref, yr_ref, cw_ref, cb_ref, wg_ref, bg_ref, lam_ref, h0_ref,
         rg_ref, fin_ref,
         xe_ref, af_ref, uf_ref, ab_ref, ub_ref, sh_ref, fl_ref, pe_ref, cinf_ref, cinb_ref) = refs
    else:
        (xr_ref, cw_ref, cb_ref, wg_ref, bg_ref, lam_ref, h0_ref,
         fin_ref,
         xe_ref, af_ref, uf_ref, ab_ref, ub_ref, sh_ref, fl_ref, pe_ref, cinf_ref, cinb_ref) = refs
    R, W = rows, width
    hd = LRU_HEAD_DIM
    rb = tb // W

    xe_ref[2:2 + R] = xr_ref[...].astype(F32).reshape(R, W, hd)
    zero8 = jnp.zeros((8, hd), F32)
    sh_ref[0:8, :] = zero8
    sh_ref[8 + W:16 + W, :] = zero8
    for src, dst, off in ((R - 2, 0, 7), (R - 1, 1, 7), (0, R + 2, 9)):
        sh_ref[8:8 + W, :] = xe_ref[2 + src]
        xe_ref[dst] = sh_ref[off:off + W, :]

    a_scale = _softplus(-lam_ref[...]) * (-0.5 * LRU_C * LOG2E)
    cwv = cw_ref[...]
    cbv = cb_ref[...]
    wg = wg_ref[...]
    bg = bg_ref[...]

    def gate_body(blk, carry):
        r0 = blk * rb
        xc = cbv
        for k in range(CONV_W):
            xc = xc + cwv[k:k + 1, :] * xe_ref[pl.ds(r0 + k, rb)]
        xc2 = xc.reshape(rb * W, hd)
        pre = jnp.dot(xc2.astype(BF16), wg, preferred_element_type=F32) + bg
        half_x = 0.5 * xc2
        for d, (a_ref, u_ref) in enumerate(((af_ref, uf_ref), (ab_ref, ub_ref))):
            t_r = jnp.tanh(pre[:, (2 * d) * hd:(2 * d + 1) * hd])
            t_i = jnp.tanh(pre[:, (2 * d + 1) * hd:(2 * d + 2) * hd])
            k = a_scale[d:d + 1, :]
            a = jnp.exp2(t_r * k + k)
            v = 1.0 - a * a
            u = (v * lax.rsqrt(jnp.maximum(v, 1e-37))) * (t_i * half_x + half_x)
            a_ref[pl.ds(r0, rb)] = a.reshape(rb, W, hd)
            u_ref[pl.ds(r0, rb)] = u.reshape(rb, W, hd)
        return carry

    lax.fori_loop(0, R // rb, gate_body, 0)

    zeros = jnp.zeros((W, hd), F32)
    ones = jnp.ones((W, hd), F32)

    def scan_step(a_ref, u_ref, r, h, p):
        a = a_ref[r]
        h = a * h + u_ref[r]
        p = a * p
        u_ref[r] = h
        a_ref[r] = p
        return h, p

    def scan_body(t, carry):
        hf, pf, hb, pb = carry
        hf, pf = scan_step(af_ref, uf_ref, t, hf, pf)
        hb, pb = scan_step(ab_ref, ub_ref, R - 1 - t, hb, pb)
        return hf, pf, hb, pb

    def carry_chain(h_end, p_end, h0, cin_ref, reverse):
        fl_ref[...] = h_end
        pe_ref[...] = p_end

        def body(t, carry):
            cc = (W - 1 - t) if reverse else t
            cin_ref[pl.ds(cc, 1), :] = carry
            return fl_ref[pl.ds(cc, 1), :] + pe_ref[pl.ds(cc, 1), :] * carry
        return lax.fori_loop(0, W, body, h0)

    hf_end, pf_end, hb_end, pb_end = lax.fori_loop(0, R, scan_body, (zeros, ones, zeros, ones), unroll=4)
    fin_f = carry_chain(hf_end, pf_end, h0_ref[0:1, :], cinf_ref, False)
    fin_b = carry_chain(hb_end, pb_end, h0_ref[1:2, :], cinb_ref, True)
    fin_ref[0:1, :] = fin_f
    fin_ref[1:2, :] = fin_b

    if emit:
        cin_f = cinf_ref[...]
        cin_b = cinb_ref[...]

        def out_body(r, carry):
            h = (uf_ref[r] + af_ref[r] * cin_f) + (ub_ref[r] + ab_ref[r] * cin_b)
            row0 = pl.multiple_of(r * W, W)
            y = yr_ref[pl.ds(row0, W), :].astype(F32)
            rg_ref[pl.ds(row0, W), :] = (h * _gelu_tanh(y)).astype(BF16)
            return carry

        lax.fori_loop(0, R, out_body, 0, unroll=2)


def _lru(xr_src, xr_col, yr_src, yr_col, cw, cb, wg, bg, lam, h0, *, rows, width, emit, hps=1):
    bsz, s, _ = xr_src.shape
    hd = hps * LRU_HEAD_DIM
    tb = min(512, s)
    kern = functools.partial(_lru_kernel, rows=rows, width=width, tb=tb, emit=emit, hps=hps)
    head = lambda b, h: (0, h)
    in_specs = [pl.BlockSpec((None, s, hd), lambda b, h: (b, 0, xr_col // hd + h))]
    args = [xr_src]
    if emit:
        in_specs.append(pl.BlockSpec((None, s, hd), lambda b, h: (b, 0, yr_col // hd + h)))
        args.append(yr_src)
    in_specs += [pl.BlockSpec((CONV_W, hd), head),
                 pl.BlockSpec((1, hd), head),
                 pl.BlockSpec((hps, LRU_HEAD_DIM, 4 * LRU_HEAD_DIM), lambda b, h: (h, 0, 0)),
                 pl.BlockSpec((hps, 1, 4 * LRU_HEAD_DIM), lambda b, h: (h, 0, 0)),
                 pl.BlockSpec((2, hd), head),
                 pl.BlockSpec((None, 2, hd), lambda b, h: (b, 0, h))]
    args += [cw, cb, wg, bg, lam, h0]
    out_shape = [jax.ShapeDtypeStruct((bsz, 2, LRU_WIDTH), F32)]
    out_specs = [pl.BlockSpec((None, 2, hd), lambda b, h: (b, 0, h))]
    if emit:
        out_shape.insert(0, jax.ShapeDtypeStruct((bsz, s, LRU_WIDTH), BF16))
        out_specs.insert(0, pl.BlockSpec((None, s, hd), lambda b, h: (b, 0, h)))
    hl = LRU_HEAD_DIM
    big = pltpu.VMEM((rows, width, hl), F32)
    scratch = [pltpu.VMEM((rows + 3, width, hl), F32), big, big, big, big,
               pltpu.VMEM((width + 16, hl), F32),
               pltpu.VMEM((width, hl), F32), pltpu.VMEM((width, hl), F32),
               pltpu.VMEM((width, hl), F32), pltpu.VMEM((width, hl), F32)]
    return pl.pallas_call(
        kern,
        out_shape=tuple(out_shape),
        grid=(bsz, LRU_HEADS // hps),
        in_specs=in_specs,
        out_specs=tuple(out_specs),
        scratch_shapes=scratch,
        compiler_params=_params(("arbitrary", "arbitrary")),
        name="lru" + ("" if emit else "_ctx"),
    )(*args)


def _merge_a_kernel(yn_ref, rg_ref, ws_ref, wl_ref, gs0_ref, gs1_ref, gr0_ref, gr1_ref, m_ref):
    yn = yn_ref[...]
    rg = rg_ref[...]
    for j in range(0, m_ref.shape[1], MXU_COLS):
        cols = slice(j, j + MXU_COLS)
        gcols = slice(j % PROJ_TN, j % PROJ_TN + MXU_COLS)
        gs_ref, gr_ref = (gs0_ref, gr0_ref) if j < PROJ_TN else (gs1_ref, gr1_ref)
        o_s = jnp.dot(yn, ws_ref[:, cols], preferred_element_type=F32)
        o_r = jnp.dot(rg, wl_ref[:, cols], preferred_element_type=F32)
        m_ref[:, cols] = (gs_ref[:, gcols].astype(F32) * o_s + gr_ref[:, gcols].astype(F32) * o_r).astype(BF16)


def _merge_a(yn, rg, ws, wl, p, tm):
    bsz, s, d = yn.shape
    row = lambda b, i: (b, i, 0)
    whole = lambda b, i: (0, 0)
    return pl.pallas_call(
        _merge_a_kernel,
        out_shape=jax.ShapeDtypeStruct((bsz, s, d), BF16),
        grid=(bsz, s // tm),
        in_specs=[pl.BlockSpec((None, tm, d), row),
                  pl.BlockSpec((None, tm, d), row),
                  pl.BlockSpec((d, d), whole),
                  pl.BlockSpec((d, d), whole),
                  pl.BlockSpec((None, tm, PROJ_TN), lambda b, i: (b, i, COL_GS // PROJ_TN)),
                  pl.BlockSpec((None, tm, PROJ_TN), lambda b, i: (b, i, COL_GS // PROJ_TN + 1)),
                  pl.BlockSpec((None, tm, PROJ_TN), lambda b, i: (b, i, COL_GR // PROJ_TN)),
                  pl.BlockSpec((None, tm, PROJ_TN), lambda b, i: (b, i, COL_GR // PROJ_TN + 1))],
        out_specs=pl.BlockSpec((None, tm, d), row),
        compiler_params=_params(("arbitrary", "arbitrary")),
        name="merge_a",
    )(yn, rg, ws, wl, p, p, p, p)


def _merge_b_kernel(m_ref, w_ref, x_ref, gm_ref, sh_ref, sc_ref, g_ref, x1_ref, h_ref):
    half = m_ref.shape[0] // 2
    for r in (0, half):
        rows = slice(r, r + half)
        mix = jnp.dot(m_ref[rows, :], w_ref[...], preferred_element_type=F32)
        x1 = x_ref[rows, :] + gm_ref[...] * mix
        x1_ref[rows, :] = x1
        h_ref[rows, :] = _norm_mod(x1, g_ref[...], sh_ref[...], sc_ref[...]).astype(BF16)


def _merge_b(m, w_o, x, gm, shift, scale, g, tm):
    bsz, s, d = m.shape
    row = lambda b, i: (b, i, 0)
    vec = lambda b, i: (b, 0, 0)
    return pl.pallas_call(
        _merge_b_kernel,
        out_shape=(jax.ShapeDtypeStruct((bsz, s, d), F32), jax.ShapeDtypeStruct((bsz, s, d), BF16)),
        grid=(bsz, s // tm),
        in_specs=[pl.BlockSpec((None, tm, d), row),
                  pl.BlockSpec((d, d), lambda b, i: (0, 0)),
                  pl.BlockSpec((None, tm, d), row),
                  pl.BlockSpec((None, 1, d), vec),
                  pl.BlockSpec((None, 1, d), vec),
                  pl.BlockSpec((None, 1, d), vec),
                  pl.BlockSpec((1, d), lambda b, i: (0, 0))],
        out_specs=(pl.BlockSpec((None, tm, d), row), pl.BlockSpec((None, tm, d), row)),
        compiler_params=_params(("arbitrary", "arbitrary")),
        name="merge_b",
    )(m, w_o, x, gm, shift, scale, g)


def _ffn_up_kernel(h_ref, w1_ref, w3_ref, a_ref):
    h = h_ref[...]
    for j in range(0, a_ref.shape[1], MXU_COLS):
        cols = slice(j, j + MXU_COLS)
        gate = jnp.dot(h, w1_ref[:, cols], preferred_element_type=F32)
        up = jnp.dot(h, w3_ref[:, cols], preferred_element_type=F32)
        a_ref[:, cols] = (gate * _sigmoid(gate) * up).astype(BF16)


def _ffn_up(h, w13, tm, tn):
    bsz, s, d = h.shape
    hid = w13.shape[1] // 2
    nb = hid // tn
    return pl.pallas_call(
        _ffn_up_kernel,
        out_shape=jax.ShapeDtypeStruct((bsz, s, hid), BF16),
        grid=(bsz, s // tm, nb),
        in_specs=[pl.BlockSpec((None, tm, d), lambda b, i, n: (b, i, 0)),
                  pl.BlockSpec((d, tn), lambda b, i, n: (0, n)),
                  pl.BlockSpec((d, tn), lambda b, i, n: (0, nb + n))],
        out_specs=pl.BlockSpec((None, tm, tn), lambda b, i, n: (b, i, n)),
        compiler_params=_params(("arbitrary", "arbitrary", "arbitrary")),
        name="ffn_up",
    )(h, w13, w13)


def _ffn_down_kernel(a_ref, w_ref, x_ref, gf_ref, fn_ref, o_ref):
    x2 = x_ref[...] + gf_ref[...] * jnp.dot(a_ref[...], w_ref[...], preferred_element_type=F32)
    ms = jnp.mean(x2 * x2, axis=-1, keepdims=True)
    o_ref[...] = x2 * lax.rsqrt(ms + EPS) * fn_ref[...]


def _ffn_down(a, w2, x1, gf, fnorm, tm):
    bsz, s, hid = a.shape
    d = w2.shape[1]
    row = lambda b, i: (b, i, 0)
    return pl.pallas_call(
        _ffn_down_kernel,
        out_shape=jax.ShapeDtypeStruct((bsz, s, d), F32),
        grid=(bsz, s // tm),
        in_specs=[pl.BlockSpec((None, tm, hid), row),
                  pl.BlockSpec((hid, d), lambda b, i: (0, 0)),
                  pl.BlockSpec((None, tm, d), row),
                  pl.BlockSpec((None, 1, d), lambda b, i: (b, 0, 0)),
                  pl.BlockSpec((1, d), lambda b, i: (0, 0))],
        out_specs=pl.BlockSpec((None, tm, d), row),
        compiler_params=_params(("arbitrary", "arbitrary")),
        name="ffn_down",
    )(a, w2, x1, gf, fnorm)


def _pad_lanes(v, n=LANES):
    return jnp.pad(v, (0, n - v.shape[0])).reshape(1, n)


def _layer(x, ctx, mod, mod_c, prm):
    bsz, seq, d = x.shape
    clen = ctx.shape[1]
    rows = seq // GRID_W

    sh_m, sc_m, g_m, sh_f, sc_f, g_f = [m.reshape(bsz, 1, d) for m in jnp.split(mod, N_MOD, axis=-1)]
    csh_m, csc_m = [m.reshape(1, 1, d) for m in jnp.split(mod_c, N_MOD, axis=-1)[:2]]

    w_in = prm["w_in"]
    xy0 = IN_DT_END
    w_a = w_in.astype(BF16)
    w_xy = w_in[:, xy0:xy0 + 2 * LRU_WIDTH].astype(BF16)
    w_g = prm["w_gate"].astype(BF16)
    b_cat = jnp.concatenate([jnp.zeros((COL_GS,), F32), prm["b_gate"]]).reshape(1, NP)
    w_dtc = w_in[:, COL_XR:IN_DT_END]
    w_dt = jnp.zeros((d, DT_COLS), F32)
    w_dt = w_dt.at[:, 0:SSD_HEADS].set(w_dtc[:, :SSD_HEADS])
    w_dt = w_dt.at[:, LANES:LANES + SSD_HEADS].set(w_dtc[:, SSD_HEADS:]).astype(BF16)
    g_mix = prm["norm_mix"].reshape(1, d)

    p_ctx, dt_ctx = _in_proj(ctx.reshape(1, bsz * clen, d), csh_m, csc_m, g_mix, w_a, w_xy, w_g, b_cat, w_dt,
                             tm=bsz * clen, ctx_subset=True)
    p_ctx = p_ctx.reshape(bsz, clen, CTX_NP)
    dt_ctx = dt_ctx.reshape(bsz, clen, DT_COLS)
    p_lat, dt_lat = _in_proj(x, sh_m, sc_m, g_mix, w_a, w_xy, w_g, b_cat, w_dt, *TILES["in_proj"], ctx_subset=False)

    cwx = prm["ssd_conv_w"][:, :SSD_WIDTH]
    cwbc = prm["ssd_conv_w"][:, SSD_WIDTH:]
    cbx = prm["ssd_conv_b"][:SSD_WIDTH].reshape(1, -1)
    cbbc = prm["ssd_conv_b"][SSD_WIDTH:].reshape(1, -1)
    dtb = [_pad_lanes(prm["ssd_dt_bias"][k]) for k in range(2)]
    alog = [_pad_lanes(prm["ssd_a_log"][k]) for k in range(2)]
    conv = (cwx, cbx, cwbc, cbbc)
    zero_state = jnp.zeros((bsz, SSD_STATE, SSD_WIDTH), F32)
    _, s_f, xc_ctx, bc_ctx = _ssd_sweep(p_ctx, dt_ctx, zero_state, dtb[0], alog[0], rev=False,
                                        col_xs=CTX_COL_XS, col_bc=CTX_COL_BC, conv=conv)
    _, s_b = _ssd_sweep((xc_ctx, bc_ctx), dt_ctx, zero_state, dtb[1], alog[1], rev=True)
    y_f, _, xc_lat, bc_lat = _ssd_sweep(p_lat, dt_lat, s_f, dtb[0], alog[0], rev=False,
                                        col_xs=COL_XS, col_bc=COL_BC, conv=conv)
    dskip = jnp.repeat(prm["ssd_d"], SSD_HEAD_DIM).reshape(1, SSD_WIDTH)
    yn, _ = _ssd_sweep((xc_lat, bc_lat), dt_lat, s_b, dtb[1], alog[1], rev=True,
                       extra=(y_f, p_lat, dskip, prm["ssd_norm"].reshape(1, SSD_WIDTH)))

    hd = LRU_HEAD_DIM
    wg = (jnp.concatenate([prm["lru_w_a"][0], prm["lru_w_x"][0],
                           prm["lru_w_a"][1], prm["lru_w_x"][1]], axis=-1) * 0.5).astype(BF16)
    bg = jnp.stack([prm["lru_b_a"][0].reshape(LRU_HEADS, hd), prm["lru_b_x"][0].reshape(LRU_HEADS, hd),
                    prm["lru_b_a"][1].reshape(LRU_HEADS, hd), prm["lru_b_x"][1].reshape(LRU_HEADS, hd)],
                   axis=1).reshape(LRU_HEADS, 1, 4 * hd) * 0.5
    lcw = prm["lru_conv_w"]
    lcb = prm["lru_conv_b"].reshape(1, -1)
    lam = prm["lru_lambda"]
    cw_ = 8
    cr_ = clen // cw_
    xr_ctx = p_ctx[:, :, CTX_COL_XR:CTX_COL_XR + LRU_WIDTH]
    xr_ctx = xr_ctx.reshape(bsz, cw_, cr_, LRU_WIDTH).transpose(0, 2, 1, 3).reshape(bsz, clen, LRU_WIDTH)
    zero_h = jnp.zeros((bsz, 2, LRU_WIDTH), F32)
    (f_ctx,) = _lru(xr_ctx, 0, None, 0, lcw, lcb, wg, bg, lam, zero_h, rows=cr_, width=cw_, emit=False,
                    hps=LRU_CTX_HEADS_PER_STEP)
    rg, _ = _lru(p_lat, COL_XR, p_lat, COL_YR, lcw, lcb, wg, bg, lam, f_ctx,
                 rows=rows, width=GRID_W, emit=True)

    m = _merge_a(yn, rg, prm["w_out_ssd"].astype(BF16), prm["w_out_lru"].astype(BF16), p_lat, *TILES["merge_a"])
    x1, h_ffn = _merge_b(m, prm["w_o"].astype(BF16), x, g_m, sh_f, sc_f, prm["norm_ffn"].reshape(1, d),
                         *TILES["merge_b"])

    act = _ffn_up(h_ffn, prm["ffn_w13"].astype(BF16), *TILES["ffn_up"])
    return _ffn_down(act, prm["ffn_w2"].astype(BF16), x1, g_f, prm["final_norm"].reshape(1, d), *TILES["ffn_down"])


def kernel(x, c, ctx, c_ctx, w_ada, b_ada, norm_mix, norm_ffn, w_in, ssd_conv_w, ssd_conv_b, ssd_dt_bias,
           ssd_a_log, ssd_d, ssd_norm, w_out_ssd, lru_conv_w, lru_conv_b, lru_w_a, lru_b_a, lru_w_x, lru_b_x,
           lru_lambda, w_out_lru, w_gate, b_gate, w_o, ffn_w13, ffn_w2, final_norm):
    assert w_ada.shape[0] == 1, "single-layer trunk"
    bsz = x.shape[0]
    cvec = jnp.zeros((8, D_MODEL), F32).at[:bsz].set(c).at[bsz].set(c_ctx)
    mod_all = _ada(cvec, w_ada[0], b_ada[0].reshape(1, -1))
    prm = dict(w_in=w_in[0], ssd_conv_w=ssd_conv_w[0], ssd_conv_b=ssd_conv_b[0], ssd_dt_bias=ssd_dt_bias[0],
               ssd_a_log=ssd_a_log[0], ssd_d=ssd_d[0], ssd_norm=ssd_norm[0], w_out_ssd=w_out_ssd[0],
               lru_conv_w=lru_conv_w[0], lru_conv_b=lru_conv_b[0], lru_w_a=lru_w_a[0], lru_b_a=lru_b_a[0],
               lru_w_x=lru_w_x[0], lru_b_x=lru_b_x[0], lru_lambda=lru_lambda[0], w_out_lru=w_out_lru[0],
               w_gate=w_gate[0], b_gate=b_gate[0], w_o=w_o[0], norm_mix=norm_mix[0], norm_ffn=norm_ffn[0],
               ffn_w13=ffn_w13[0], ffn_w2=ffn_w2[0], final_norm=final_norm)
    return _layer(x, ctx, mod_all[:bsz], mod_all[bsz], prm)
```

```python
import functools

import jax
import numpy as np
import jax.numpy as jnp
from jax import lax
from jax.experimental import pallas as pl
from jax.experimental.pallas import tpu as pltpu

F32 = jnp.float32
BF16 = jnp.bfloat16

D_MODEL = 2048
GRID_W = 64
N_MOD = 6
SSD_HEAD_DIM = 64
SSD_HEADS = 32
SSD_WIDTH = SSD_HEADS * SSD_HEAD_DIM
SSD_GROUPS = 4
SSD_HPG = SSD_HEADS // SSD_GROUPS
SSD_STATE = 128
SSD_BC = SSD_GROUPS * SSD_STATE
SSD_CHUNK = 128
CONV_W = 4
CONV_PAD_LEFT = 2
LRU_WIDTH = D_MODEL
LRU_HEADS = 16
LRU_HEAD_DIM = LRU_WIDTH // LRU_HEADS
LRU_C = 8.0
EPS = 1e-6

LANES = 128
MXU_COLS = 256
LRU_CTX_HEADS_PER_STEP = 8
HALO = 16
VMEM_LIMIT = 56 * 1024 * 1024

COL_Z = 0
COL_XS = 2048
COL_BC = 4096
COL_XR = 5120
COL_YR = 7168
COL_GS = 9216
COL_GR = 11264
NP = 13312
CTX_COL_XS = 0
CTX_COL_BC = 2048
CTX_COL_XR = 3072
CTX_NP = 5120
PROJ_TN = 1024
IN_DT_END = COL_XR + 2 * SSD_HEADS
DT_COLS = 256


TILES = {"in_proj": (1024,), "merge_a": (512,), "merge_b": (512,), "ffn_up": (2048, 512), "ffn_down": (512,)}


def _params(sem, vmem=VMEM_LIMIT):
    return pltpu.CompilerParams(dimension_semantics=sem, vmem_limit_bytes=vmem)


LOG2E = 1.4426950408889634


def _sigmoid(x):
    return 1.0 / (1.0 + jnp.exp2(x * (-LOG2E)))


def _softplus(x):
    return jnp.maximum(x, 0.0) + jnp.log1p(jnp.exp(-jnp.abs(x)))


def _gelu_tanh(x):
    c1 = 0.7978845608028654
    half_x = 0.5 * x
    return half_x * jnp.tanh(x * (c1 + (c1 * 0.044715) * (x * x))) + half_x


def _norm_mod(x, g, shift, scale):
    xf = x.astype(F32)
    ms = jnp.mean(xf * xf, axis=-1, keepdims=True)
    return (xf * lax.rsqrt(ms + EPS)) * (g * (1.0 + scale)) + shift


def _ada_kernel(c_ref, w_ref, b_ref, o_ref):
    c = c_ref[...]
    s = (c * _sigmoid(c)).astype(BF16)
    o_ref[...] = jnp.dot(s, w_ref[...].astype(BF16), preferred_element_type=F32) + b_ref[...]


def _ada(cvec, w, b):
    m, d = cvec.shape
    n = w.shape[1]
    tn = 1024
    return pl.pallas_call(
        _ada_kernel,
        out_shape=jax.ShapeDtypeStruct((m, n), F32),
        grid=(n // tn,),
        in_specs=[pl.BlockSpec((m, d), lambda j: (0, 0)),
                  pl.BlockSpec((d, tn), lambda j: (0, j)),
                  pl.BlockSpec((1, tn), lambda j: (0, j))],
        out_specs=pl.BlockSpec((m, tn), lambda j: (0, j)),
        compiler_params=_params(("arbitrary",)),
        name="ada",
    )(cvec, w, b)


def _in_proj_kernel(x_ref, sh_ref, sc_ref, g_ref, w_ref, b_ref, wdt_ref, p_ref, dt_ref, hn_ref, *, gate_lo):
    n = pl.program_id(2)

    @pl.when(n == 0)
    def _():
        h = _norm_mod(x_ref[...], g_ref[...], sh_ref[...], sc_ref[...]).astype(BF16)
        hn_ref[...] = h
        dt_ref[...] = jnp.dot(h, wdt_ref[...], preferred_element_type=F32)

    def project(gate):
        hn = hn_ref[...]
        for j in range(0, p_ref.shape[1], MXU_COLS):
            acc = jnp.dot(hn, w_ref[:, j:j + MXU_COLS], preferred_element_type=F32) + b_ref[:, j:j + MXU_COLS]
            p_ref[:, j:j + MXU_COLS] = (_sigmoid(acc) if gate else acc).astype(BF16)

    if gate_lo is None:
        project(False)
        return
    is_gate = n >= gate_lo
    pl.when(is_gate)(functools.partial(project, True))
    pl.when(jnp.logical_not(is_gate))(functools.partial(project, False))


def _in_proj(x, shift, scale, g, w_cat, b_cat, w_dt, tm, ctx_subset):
    bsz, s, d = x.shape
    tn = PROJ_TN
    if ctx_subset:
        n_tiles, t_off, gate_lo = CTX_NP // tn, COL_XS // tn, None
    else:
        n_tiles, t_off, gate_lo = NP // tn, 0, COL_GS // tn
    kern = functools.partial(_in_proj_kernel, gate_lo=gate_lo)
    return pl.pallas_call(
        kern,
        out_shape=(jax.ShapeDtypeStruct((bsz, s, n_tiles * tn), BF16),
                   jax.ShapeDtypeStruct((bsz, s, DT_COLS), F32)),
        grid=(bsz, s // tm, n_tiles),
        in_specs=[pl.BlockSpec((None, tm, d), lambda b, i, n: (b, i, 0)),
                  pl.BlockSpec((None, 1, d), lambda b, i, n: (b, 0, 0)),
                  pl.BlockSpec((None, 1, d), lambda b, i, n: (b, 0, 0)),
                  pl.BlockSpec((1, d), lambda b, i, n: (0, 0)),
                  pl.BlockSpec((d, tn), lambda b, i, n: (0, n + t_off)),
                  pl.BlockSpec((1, tn), lambda b, i, n: (0, n + t_off)),
                  pl.BlockSpec((d, DT_COLS), lambda b, i, n: (0, 0))],
        out_specs=(pl.BlockSpec((None, tm, tn), lambda b, i, n: (b, i, n)),
                   pl.BlockSpec((None, tm, DT_COLS), lambda b, i, n: (b, i, 0))),
        scratch_shapes=[pltpu.VMEM((tm, d), BF16)],
        compiler_params=_params(("arbitrary", "arbitrary", "arbitrary")),
        name="in_proj",
    )(x, shift, scale, g, w_cat, b_cat, w_dt)


def _split3(v):
    v1 = v.astype(BF16)
    r1 = v - v1.astype(F32)
    v2 = r1.astype(BF16)
    r2 = r1 - v2.astype(F32)
    return v1, v2, r2.astype(BF16)


def _ssd_kernel(*refs, rev, final, nc):
    it = iter(refs)
    if rev:
        xsc_ref, bcc_ref = next(it), next(it)
    else:
        xs_ref, xsp_ref, xsn_ref, bc_ref, bcp_ref, bcn_ref = [next(it) for _ in range(6)]
        cwx_ref, cbx_ref, cwbc_ref, cbbc_ref, shc_ref, sht_ref, shb_ref = [next(it) for _ in range(7)]
    dt_ref, dtb_ref, alog_ref, tri_ref, s0_ref = [next(it) for _ in range(5)]
    if final:
        yf_ref, z_ref, dskip_ref, nw_ref = [next(it) for _ in range(4)]
    y_ref, sfin_ref = next(it), next(it)
    if not rev:
        xsc_out_ref, bcc_out_ref = next(it), next(it)
    st_ref = next(it)
    if final:
        gat_ref = next(it)

    i = pl.program_id(1)
    c = (nc - 1 - i) if rev else i
    q = SSD_CHUNK

    @pl.when(i == 0)
    def _():
        st_ref[...] = s0_ref[...]

    tri = tri_ref[...]

    if rev:
        xs = xsc_ref[...].astype(F32)
        bcv = bcc_ref[...].astype(F32)
    else:
        has_prev = c > 0
        has_next = c < nc - 1
        shift_cur = shc_ref[...]
        shift_top = sht_ref[...]
        shift_bot = shb_ref[...]

        def conv_silu(cur_ref, prev_ref, next_ref, w_ref, b_ref):
            wb = [w_ref[k:k + 1, :].astype(BF16) for k in range(CONV_W)]
            cur = cur_ref[...]
            taps = jnp.concatenate([cur * wb[k] for k in range(CONV_W)], axis=0)
            out = jnp.dot(shift_cur, taps, preferred_element_type=F32) + b_ref[...]
            prev = jnp.where(has_prev, prev_ref[...], jnp.zeros_like(prev_ref))
            top = jnp.concatenate([prev * wb[0], prev * wb[1]], axis=0)
            nxt_rows = jnp.where(has_next, next_ref[...], jnp.zeros_like(next_ref))
            bot = nxt_rows * wb[CONV_W - 1]
            head = out[0:HALO] + jnp.dot(shift_top, top, preferred_element_type=F32)
            tail = out[q - HALO:q] + jnp.dot(shift_bot, bot, preferred_element_type=F32)
            out = jnp.concatenate([head, out[HALO:q - HALO], tail], axis=0)
            return out * _sigmoid(out)

        xs = conv_silu(xs_ref, xsp_ref, xsn_ref, cwx_ref, cbx_ref)
        bcv = conv_silu(bc_ref, bcp_ref, bcn_ref, cwbc_ref, cbbc_ref)
        xsc_out_ref[...] = xs.astype(BF16)
        bcc_out_ref[...] = bcv.astype(BF16)

    dt = _softplus(dt_ref[...] + dtb_ref[...])
    a = dt * (-jnp.exp(alog_ref[...]))

    ii = lax.broadcasted_iota(jnp.int32, (q, q), 0)
    jj = lax.broadcasted_iota(jnp.int32, (q, q), 1)
    mask = (jj >= ii) if rev else (jj <= ii)

    acum = sum(jnp.dot(tri, p, preferred_element_type=F32) for p in _split3(a))
    a_t = a.T
    dt_t = dt.T
    nt = (((1,), (1,)), ((), ()))
    acum_t = sum(lax.dot_general(p, tri, nt, preferred_element_type=F32) for p in _split3(a_t))

    edge = 0 if rev else q - 1
    total = acum[edge:edge + 1, :]
    cd = jnp.exp(total)
    w_t = dt_t * jnp.exp(acum_t[:, edge:edge + 1] - acum_t)
    acum2 = acum * LOG2E
    adj2_t = acum_t * LOG2E - jnp.log2(dt_t)

    lane = lax.broadcasted_iota(jnp.int32, (q, LANES), 1)
    lo_half = lane < SSD_HEAD_DIM
    lo_row = lo_half[0:1, :]

    gsq = jnp.zeros((q, LANES), F32)
    for g in range(SSD_GROUPS):
        b_g = bcv[:, g * SSD_STATE:(g + 1) * SSD_STATE]
        c_g = bcv[:, SSD_BC + g * SSD_STATE:SSD_BC + (g + 1) * SSD_STATE]
        b_gb = b_g.astype(BF16)
        c_gb = c_g.astype(BF16)
        cb = lax.dot_general(c_gb, b_gb, nt, preferred_element_type=F32)
        gw = SSD_HPG * SSD_HEAD_DIM
        s_in = st_ref[:, g * gw:(g + 1) * gw].astype(BF16)
        yoff = jnp.dot(c_gb, s_in, preferred_element_type=F32)
        b_t = b_g.T
        for kp in range(SSD_HPG // 2):
            pair = g * (SSD_HPG // 2) + kp
            col = pair * LANES
            ms, ecols, bws, cds = [], [], [], []
            for e in (2 * pair, 2 * pair + 1):
                colb = jnp.broadcast_to(acum2[:, e:e + 1], (q, q))
                seg = colb - adj2_t[e:e + 1, :]
                lmat_dt = jnp.exp2(jnp.where(mask, seg, -1e30))
                ms.append((cb * lmat_dt).astype(BF16))
                ecols.append(jnp.exp2(colb))
                bws.append((b_t * w_t[e:e + 1, :]).astype(BF16))
                cds.append(jnp.broadcast_to(cd[:, e:e + 1], (1, LANES)))
            xp = xs[:, col:col + LANES]
            rhs = jnp.concatenate([jnp.where(lo_half, xp, 0.0).astype(BF16),
                                   jnp.where(lo_half, 0.0, xp).astype(BF16)], axis=0)
            lhs = jnp.concatenate(ms, axis=1)
            ydiag = jnp.dot(lhs, rhs, preferred_element_type=F32)
            y_pair = ydiag + jnp.where(lo_half, ecols[0], ecols[1]) * yoff[:, kp * LANES:(kp + 1) * LANES]
            cd_pair = jnp.where(lo_row, cds[0], cds[1])
            ds = jnp.dot(jnp.concatenate(bws, axis=1), rhs, preferred_element_type=F32)
            st_ref[:, col:col + LANES] = st_ref[:, col:col + LANES] * cd_pair + ds
            if final:
                y_tot = y_pair + yf_ref[:, col:col + LANES] + dskip_ref[:, col:col + LANES] * xp
                zz = z_ref[:, col:col + LANES].astype(F32)
                gated = y_tot * (zz * _sigmoid(zz))
                gat_ref[:, col:col + LANES] = gated
                gsq = gsq + gated * gated
            else:
                y_ref[:, col:col + LANES] = y_pair

    if final:
        ms = jnp.sum(gsq, axis=-1, keepdims=True) * (1.0 / SSD_WIDTH)
        y_ref[...] = (gat_ref[...] * lax.rsqrt(ms + EPS) * nw_ref[...]).astype(BF16)

    @pl.when(i == nc - 1)
    def _():
        sfin_ref[...] = st_ref[...]


def _ssd_constants(rev):
    q = SSD_CHUNK
    t = np.arange(q)[:, None]
    r = np.arange(CONV_W * q)[None, :]
    shift_cur = (r % q) == t + r // q - CONV_PAD_LEFT
    shift_top = np.zeros((HALO, 2 * HALO), bool)
    shift_top[0, HALO - 2] = shift_top[1, HALO - 1] = shift_top[0, 2 * HALO - 1] = True
    shift_bot = np.zeros((HALO, HALO), bool)
    shift_bot[HALO - 1, 0] = True
    i = np.arange(q)[:, None]
    j = np.arange(q)[None, :]
    tri = (j >= i) if rev else (j <= i)
    return [jnp.asarray(m, BF16) for m in (shift_cur, shift_top, shift_bot, tri)]


def _ssd_sweep(src, dtraw, s0, dtb, alog, *, rev, col_xs=None, col_bc=None, conv=None, extra=None):
    bsz, s, _ = dtraw.shape
    nc = s // SSD_CHUNK
    q = SSD_CHUNK
    hb = q // HALO
    final = extra is not None
    assert rev or not final

    def cidx(i):
        return (nc - 1 - i) if rev else i

    def cur(col_block):
        return lambda b, i: (b, cidx(i), col_block)

    def prev(col_block):
        return lambda b, i: (b, jnp.maximum(cidx(i) * hb - 1, 0), col_block)

    def nxt(col_block):
        return lambda b, i: (b, jnp.minimum(cidx(i) * hb + hb, s // HALO - 1), col_block)

    const2 = lambda b, i: (0, 0)
    xw_, bw_ = SSD_WIDTH, 2 * SSD_BC
    shift_cur, shift_top, shift_bot, tri = _ssd_constants(rev)
    if rev:
        in_specs = [pl.BlockSpec((None, q, xw_), cur(0)), pl.BlockSpec((None, q, bw_), cur(0))]
        args = list(src)
    else:
        p = src
        in_specs = [pl.BlockSpec((None, q, xw_), cur(col_xs // xw_)),
                    pl.BlockSpec((None, HALO, xw_), prev(col_xs // xw_)),
                    pl.BlockSpec((None, HALO, xw_), nxt(col_xs // xw_)),
                    pl.BlockSpec((None, q, bw_), cur(col_bc // bw_)),
                    pl.BlockSpec((None, HALO, bw_), prev(col_bc // bw_)),
                    pl.BlockSpec((None, HALO, bw_), nxt(col_bc // bw_)),
                    pl.BlockSpec((CONV_W, xw_), const2),
                    pl.BlockSpec((1, xw_), const2),
                    pl.BlockSpec((CONV_W, bw_), const2),
                    pl.BlockSpec((1, bw_), const2),
                    pl.BlockSpec((q, CONV_W * q), const2),
                    pl.BlockSpec((HALO, 2 * HALO), const2),
                    pl.BlockSpec((HALO, HALO), const2)]
        args = [p, p, p, p, p, p, *conv, shift_cur, shift_top, shift_bot]
    in_specs += [pl.BlockSpec((None, q, LANES), cur(1 if rev else 0)),
                 pl.BlockSpec((1, LANES), const2),
                 pl.BlockSpec((1, LANES), const2),
                 pl.BlockSpec((q, q), const2),
                 pl.BlockSpec((None, SSD_STATE, xw_), lambda b, i: (b, 0, 0))]
    args += [dtraw, dtb, alog, tri, s0]
    scratch = [pltpu.VMEM((SSD_STATE, xw_), F32)]
    if final:
        yf, z_src, dskip, nw = extra
        in_specs += [pl.BlockSpec((None, q, xw_), cur(0)),
                     pl.BlockSpec((None, q, xw_), cur(COL_Z // xw_)),
                     pl.BlockSpec((1, xw_), const2),
                     pl.BlockSpec((1, xw_), const2)]
        args += [yf, z_src, dskip, nw]
        scratch.append(pltpu.VMEM((q, xw_), F32))
    out_shape = [jax.ShapeDtypeStruct((bsz, s, xw_), BF16 if final else F32),
                 jax.ShapeDtypeStruct((bsz, SSD_STATE, xw_), F32)]
    out_specs = [pl.BlockSpec((None, q, xw_), cur(0)),
                 pl.BlockSpec((None, SSD_STATE, xw_), lambda b, i: (b, 0, 0))]
    if not rev:
        out_shape += [jax.ShapeDtypeStruct((bsz, s, xw_), BF16), jax.ShapeDtypeStruct((bsz, s, bw_), BF16)]
        out_specs += [pl.BlockSpec((None, q, xw_), cur(0)), pl.BlockSpec((None, q, bw_), cur(0))]
    kern = functools.partial(_ssd_kernel, rev=rev, final=final, nc=nc)
    return pl.pallas_call(
        kern,
        out_shape=tuple(out_shape),
        grid=(bsz, nc),
        in_specs=in_specs,
        out_specs=tuple(out_specs),
        scratch_shapes=scratch,
        compiler_params=_params(("arbitrary", "arbitrary")),
        name="ssd_" + ("bwd" if rev else "fwd") + ("_final" if final else ""),
    )(*args)


def _lru_kernel(*refs, rows, width, tb, emit, hps):
    n_in = 8 if emit else 7
    n_out = 2 if emit else 1
    per_head_leading = (4, 5) if emit else (3, 4)
    for hh in range(hps):
        lanes = slice(hh * LRU_HEAD_DIM, (hh + 1) * LRU_HEAD_DIM)
        views = [r.at[hh] if k in per_head_leading else r.at[:, lanes] for k, r in enumerate(refs[:n_in + n_out])]
        _lru_head(*views, *refs[n_in + n_out:], rows=rows, width=width, tb=tb, emit=emit)


def _lru_head(*refs, rows, width, tb, emit):
    if emit:
        (xr_ref, yr_ref, cw_ref, cb_ref, wg_ref, bg_ref, lam_ref, h0_ref,
         rg_ref, fin_ref,
         xe_ref, af_ref, uf_ref, ab_ref, ub_ref, sh_ref, fl_ref, pe_ref, cinf_ref, cinb_ref) = refs
    else:
        (xr_ref, cw_ref, cb_ref, wg_ref, bg_ref, lam_ref, h0_ref,
         fin_ref,
         xe_ref, af_ref, uf_ref, ab_ref, ub_ref, sh_ref, fl_ref, pe_ref, cinf_ref, cinb_ref) = refs
    R, W = rows, width
    hd = LRU_HEAD_DIM
    rb = tb // W

    xe_ref[2:2 + R] = xr_ref[...].astype(F32).reshape(R, W, hd)
    zero8 = jnp.zeros((8, hd), F32)
    sh_ref[0:8, :] = zero8
    sh_ref[8 + W:16 + W, :] = zero8
    for src, dst, off in ((R - 2, 0, 7), (R - 1, 1, 7), (0, R + 2, 9)):
        sh_ref[8:8 + W, :] = xe_ref[2 + src]
        xe_ref[dst] = sh_ref[off:off + W, :]

    a_scale = _softplus(-lam_ref[...]) * (-0.5 * LRU_C * LOG2E)
    cwv = cw_ref[...]
    cbv = cb_ref[...]
    wg = wg_ref[...]
    bg = bg_ref[...]

    def gate_body(blk, carry):
        r0 = blk * rb
        xc = cbv
        for k in range(CONV_W):
            xc = xc + cwv[k:k + 1, :] * xe_ref[pl.ds(r0 + k, rb)]
        xc2 = xc.reshape(rb * W, hd)
        pre = jnp.dot(xc2.astype(BF16), wg, preferred_element_type=F32) + bg
        half_x = 0.5 * xc2
        for d, (a_ref, u_ref) in enumerate(((af_ref, uf_ref), (ab_ref, ub_ref))):
            t_r = jnp.tanh(pre[:, (2 * d) * hd:(2 * d + 1) * hd])
            t_i = jnp.tanh(pre[:, (2 * d + 1) * hd:(2 * d + 2) * hd])
            k = a_scale[d:d + 1, :]
            a = jnp.exp2(t_r * k + k)
            v = 1.0 - a * a
            u = (v * lax.rsqrt(jnp.maximum(v, 1e-37))) * (t_i * half_x + half_x)
            a_ref[pl.ds(r0, rb)] = a.reshape(rb, W, hd)
            u_ref[pl.ds(r0, rb)] = u.reshape(rb, W, hd)
        return carry

    lax.fori_loop(0, R // rb, gate_body, 0)

    zeros = jnp.zeros((W, hd), F32)
    ones = jnp.ones((W, hd), F32)

    def scan_step(a_ref, u_ref, r, h, p):
        a = a_ref[r]
        h = a * h + u_ref[r]
        p = a * p
        u_ref[r] = h
        a_ref[r] = p
        return h, p

    def scan_body(t, carry):
        hf, pf, hb, pb = carry
        hf, pf = scan_step(af_ref, uf_ref, t, hf, pf)
        hb, pb = scan_step(ab_ref, ub_ref, R - 1 - t, hb, pb)
        return hf, pf, hb, pb

    def carry_chain(h_end, p_end, h0, cin_ref, reverse):
        fl_ref[...] = h_end
        pe_ref[...] = p_end

        def body(t, carry):
            cc = (W - 1 - t) if reverse else t
            cin_ref[pl.ds(cc, 1), :] = carry
            return fl_ref[pl.ds(cc, 1), :] + pe_ref[pl.ds(cc, 1), :] * carry
        return lax.fori_loop(0, W, body, h0)

    hf_end, pf_end, hb_end, pb_end = lax.fori_loop(0, R, scan_body, (zeros, ones, zeros, ones), unroll=4)
    fin_f = carry_chain(hf_end, pf_end, h0_ref[0:1, :], cinf_ref, False)
    fin_b = carry_chain(hb_end, pb_end, h0_ref[1:2, :], cinb_ref, True)
    fin_ref[0:1, :] = fin_f
    fin_ref[1:2, :] = fin_b

    if emit:
        cin_f = cinf_ref[...]
        cin_b = cinb_ref[...]

        def out_body(r, carry):
            h = (uf_ref[r] + af_ref[r] * cin_f) + (ub_ref[r] + ab_ref[r] * cin_b)
            row0 = pl.multiple_of(r * W, W)
            y = yr_ref[pl.ds(row0, W), :].astype(F32)
            rg_ref[pl.ds(row0, W), :] = (h * _gelu_tanh(y)).astype(BF16)
            return carry

        lax.fori_loop(0, R, out_body, 0, unroll=2)


def _lru(xr_src, xr_col, yr_src, yr_col, cw, cb, wg, bg, lam, h0, *, rows, width, emit, hps=1):
    bsz, s, _ = xr_src.shape
    hd = hps * LRU_HEAD_DIM
    tb = min(512, s)
    kern = functools.partial(_lru_kernel, rows=rows, width=width, tb=tb, emit=emit, hps=hps)
    head = lambda b, h: (0, h)
    in_specs = [pl.BlockSpec((None, s, hd), lambda b, h: (b, 0, xr_col // hd + h))]
    args = [xr_src]
    if emit:
        in_specs.append(pl.BlockSpec((None, s, hd), lambda b, h: (b, 0, yr_col // hd + h)))
        args.append(yr_src)
    in_specs += [pl.BlockSpec((CONV_W, hd), head),
                 pl.BlockSpec((1, hd), head),
                 pl.BlockSpec((hps, LRU_HEAD_DIM, 4 * LRU_HEAD_DIM), lambda b, h: (h, 0, 0)),
                 pl.BlockSpec((hps, 1, 4 * LRU_HEAD_DIM), lambda b, h: (h, 0, 0)),
                 pl.BlockSpec((2, hd), head),
                 pl.BlockSpec((None, 2, hd), lambda b, h: (b, 0, h))]
    args += [cw, cb, wg, bg, lam, h0]
    out_shape = [jax.ShapeDtypeStruct((bsz, 2, LRU_WIDTH), F32)]
    out_specs = [pl.BlockSpec((None, 2, hd), lambda b, h: (b, 0, h))]
    if emit:
        out_shape.insert(0, jax.ShapeDtypeStruct((bsz, s, LRU_WIDTH), BF16))
        out_specs.insert(0, pl.BlockSpec((None, s, hd), lambda b, h: (b, 0, h)))
    hl = LRU_HEAD_DIM
    big = pltpu.VMEM((rows, width, hl), F32)
    scratch = [pltpu.VMEM((rows + 3, width, hl), F32), big, big, big, big,
               pltpu.VMEM((width + 16, hl), F32),
               pltpu.VMEM((width, hl), F32), pltpu.VMEM((width, hl), F32),
               pltpu.VMEM((width, hl), F32), pltpu.VMEM((width, hl), F32)]
    return pl.pallas_call(
        kern,
        out_shape=tuple(out_shape),
        grid=(bsz, LRU_HEADS // hps),
        in_specs=in_specs,
        out_specs=tuple(out_specs),
        scratch_shapes=scratch,
        compiler_params=_params(("arbitrary", "arbitrary")),
        name="lru" + ("" if emit else "_ctx"),
    )(*args)


def _merge_a_kernel(yn_ref, rg_ref, ws_ref, wl_ref, gs0_ref, gs1_ref, gr0_ref, gr1_ref, m_ref):
    yn = yn_ref[...]
    rg = rg_ref[...]
    for j in range(0, m_ref.shape[1], MXU_COLS):
        cols = slice(j, j + MXU_COLS)
        gcols = slice(j % PROJ_TN, j % PROJ_TN + MXU_COLS)
        gs_ref, gr_ref = (gs0_ref, gr0_ref) if j < PROJ_TN else (gs1_ref, gr1_ref)
        o_s = jnp.dot(yn, ws_ref[:, cols], preferred_element_type=F32)
        o_r = jnp.dot(rg, wl_ref[:, cols], preferred_element_type=F32)
        m_ref[:, cols] = (gs_ref[:, gcols].astype(F32) * o_s + gr_ref[:, gcols].astype(F32) * o_r).astype(BF16)


def _merge_a(yn, rg, ws, wl, p, tm):
    bsz, s, d = yn.shape
    row = lambda b, i: (b, i, 0)
    whole = lambda b, i: (0, 0)
    return pl.pallas_call(
        _merge_a_kernel,
        out_shape=jax.ShapeDtypeStruct((bsz, s, d), BF16),
        grid=(bsz, s // tm),
        in_specs=[pl.BlockSpec((None, tm, d), row),
                  pl.BlockSpec((None, tm, d), row),
                  pl.BlockSpec((d, d), whole),
                  pl.BlockSpec((d, d), whole),
                  pl.BlockSpec((None, tm, PROJ_TN), lambda b, i: (b, i, COL_GS // PROJ_TN)),
                  pl.BlockSpec((None, tm, PROJ_TN), lambda b, i: (b, i, COL_GS // PROJ_TN + 1)),
                  pl.BlockSpec((None, tm, PROJ_TN), lambda b, i: (b, i, COL_GR // PROJ_TN)),
                  pl.BlockSpec((None, tm, PROJ_TN), lambda b, i: (b, i, COL_GR // PROJ_TN + 1))],
        out_specs=pl.BlockSpec((None, tm, d), row),
        compiler_params=_params(("arbitrary", "arbitrary")),
        name="merge_a",
    )(yn, rg, ws, wl, p, p, p, p)


def _merge_b_kernel(m_ref, w_ref, x_ref, gm_ref, sh_ref, sc_ref, g_ref, x1_ref, h_ref):
    half = m_ref.shape[0] // 2
    for r in (0, half):
        rows = slice(r, r + half)
        mix = jnp.dot(m_ref[rows, :], w_ref[...], preferred_element_type=F32)
        x1 = x_ref[rows, :] + gm_ref[...] * mix
        x1_ref[rows, :] = x1
        h_ref[rows, :] = _norm_mod(x1, g_ref[...], sh_ref[...], sc_ref[...]).astype(BF16)


def _merge_b(m, w_o, x, gm, shift, scale, g, tm):
    bsz, s, d = m.shape
    row = lambda b, i: (b, i, 0)
    vec = lambda b, i: (b, 0, 0)
    return pl.pallas_call(
        _merge_b_kernel,
        out_shape=(jax.ShapeDtypeStruct((bsz, s, d), F32), jax.ShapeDtypeStruct((bsz, s, d), BF16)),
        grid=(bsz, s // tm),
        in_specs=[pl.BlockSpec((None, tm, d), row),
                  pl.BlockSpec((d, d), lambda b, i: (0, 0)),
                  pl.BlockSpec((None, tm, d), row),
                  pl.BlockSpec((None, 1, d), vec),
                  pl.BlockSpec((None, 1, d), vec),
                  pl.BlockSpec((None, 1, d), vec),
                  pl.BlockSpec((1, d), lambda b, i: (0, 0))],
        out_specs=(pl.BlockSpec((None, tm, d), row), pl.BlockSpec((None, tm, d), row)),
        compiler_params=_params(("arbitrary", "arbitrary")),
        name="merge_b",
    )(m, w_o, x, gm, shift, scale, g)


def _ffn_up_kernel(h_ref, w1_ref, w3_ref, a_ref):
    h = h_ref[...]
    for j in range(0, a_ref.shape[1], MXU_COLS):
        cols = slice(j, j + MXU_COLS)
        gate = jnp.dot(h, w1_ref[:, cols], preferred_element_type=F32)
        up = jnp.dot(h, w3_ref[:, cols], preferred_element_type=F32)
        a_ref[:, cols] = (gate * _sigmoid(gate) * up).astype(BF16)


def _ffn_up(h, w13, tm, tn):
    bsz, s, d = h.shape
    hid = w13.shape[1] // 2
    nb = hid // tn
    return pl.pallas_call(
        _ffn_up_kernel,
        out_shape=jax.ShapeDtypeStruct((bsz, s, hid), BF16),
        grid=(bsz, s // tm, nb),
        in_specs=[pl.BlockSpec((None, tm, d), lambda b, i, n: (b, i, 0)),
                  pl.BlockSpec((d, tn), lambda b, i, n: (0, n)),
                  pl.BlockSpec((d, tn), lambda b, i, n: (0, nb + n))],
        out_specs=pl.BlockSpec((None, tm, tn), lambda b, i, n: (b, i, n)),
        compiler_params=_params(("arbitrary", "arbitrary", "arbitrary")),
        name="ffn_up",
    )(h, w13, w13)


def _ffn_down_kernel(a_ref, w_ref, x_ref, gf_ref, fn_ref, o_ref):
    x2 = x_ref[...] + gf_ref[...] * jnp.dot(a_ref[...], w_ref[...], preferred_element_type=F32)
    ms = jnp.mean(x2 * x2, axis=-1, keepdims=True)
    o_ref[...] = x2 * lax.rsqrt(ms + EPS) * fn_ref[...]


def _ffn_down(a, w2, x1, gf, fnorm, tm):
    bsz, s, hid = a.shape
    d = w2.shape[1]
    row = lambda b, i: (b, i, 0)
    return pl.pallas_call(
        _ffn_down_kernel,
        out_shape=jax.ShapeDtypeStruct((bsz, s, d), F32),
        grid=(bsz, s // tm),
        in_specs=[pl.BlockSpec((None, tm, hid), row),
                  pl.BlockSpec((hid, d), lambda b, i: (0, 0)),
                  pl.BlockSpec((None, tm, d), row),
                  pl.BlockSpec((None, 1, d), lambda b, i: (b, 0, 0)),
                  pl.BlockSpec((1, d), lambda b, i: (0, 0))],
        out_specs=pl.BlockSpec((None, tm, d), row),
        compiler_params=_params(("arbitrary", "arbitrary")),
        name="ffn_down",
    )(a, w2, x1, gf, fnorm)


def _pad_lanes(v, n=LANES):
    return jnp.pad(v, (0, n - v.shape[0])).reshape(1, n)


def _layer(x, ctx, mod, mod_c, prm):
    bsz, seq, d = x.shape
    clen = ctx.shape[1]
    rows = seq // GRID_W

    sh_m, sc_m, g_m, sh_f, sc_f, g_f = [m.reshape(bsz, 1, d) for m in jnp.split(mod, N_MOD, axis=-1)]
    csh_m, csc_m = [m.reshape(1, 1, d) for m in jnp.split(mod_c, N_MOD, axis=-1)[:2]]

    w_in = prm["w_in"]
    w_cat = jnp.concatenate([w_in[:, :COL_XR].astype(BF16),
                             w_in[:, IN_DT_END:IN_DT_END + 2 * LRU_WIDTH].astype(BF16),
                             prm["w_gate"].astype(BF16)], axis=1)
    b_cat = jnp.concatenate([jnp.zeros((COL_GS,), F32), prm["b_gate"]]).reshape(1, NP)
    w_dtc = w_in[:, COL_XR:IN_DT_END]
    w_dt = jnp.zeros((d, DT_COLS), F32)
    w_dt = w_dt.at[:, 0:SSD_HEADS].set(w_dtc[:, :SSD_HEADS])
    w_dt = w_dt.at[:, LANES:LANES + SSD_HEADS].set(w_dtc[:, SSD_HEADS:]).astype(BF16)
    g_mix = prm["norm_mix"].reshape(1, d)

    p_ctx, dt_ctx = _in_proj(ctx.reshape(1, bsz * clen, d), csh_m, csc_m, g_mix, w_cat, b_cat, w_dt,
                             tm=bsz * clen, ctx_subset=True)
    p_ctx = p_ctx.reshape(bsz, clen, CTX_NP)
    dt_ctx = dt_ctx.reshape(bsz, clen, DT_COLS)
    p_lat, dt_lat = _in_proj(x, sh_m, sc_m, g_mix, w_cat, b_cat, w_dt, *TILES["in_proj"], ctx_subset=False)

    cwx = prm["ssd_conv_w"][:, :SSD_WIDTH]
    cwbc = prm["ssd_conv_w"][:, SSD_WIDTH:]
    cbx = prm["ssd_conv_b"][:SSD_WIDTH].reshape(1, -1)
    cbbc = prm["ssd_conv_b"][SSD_WIDTH:].reshape(1, -1)
    dtb = [_pad_lanes(prm["ssd_dt_bias"][k]) for k in range(2)]
    alog = [_pad_lanes(prm["ssd_a_log"][k]) for k in range(2)]
    conv = (cwx, cbx, cwbc, cbbc)
    zero_state = jnp.zeros((bsz, SSD_STATE, SSD_WIDTH), F32)
    _, s_f, xc_ctx, bc_ctx = _ssd_sweep(p_ctx, dt_ctx, zero_state, dtb[0], alog[0], rev=False,
                                        col_xs=CTX_COL_XS, col_bc=CTX_COL_BC, conv=conv)
    _, s_b = _ssd_sweep((xc_ctx, bc_ctx), dt_ctx, zero_state, dtb[1], alog[1], rev=True)
    y_f, _, xc_lat, bc_lat = _ssd_sweep(p_lat, dt_lat, s_f, dtb[0], alog[0], rev=False,
                                        col_xs=COL_XS, col_bc=COL_BC, conv=conv)
    dskip = jnp.repeat(prm["ssd_d"], SSD_HEAD_DIM).reshape(1, SSD_WIDTH)
    yn, _ = _ssd_sweep((xc_lat, bc_lat), dt_lat, s_b, dtb[1], alog[1], rev=True,
                       extra=(y_f, p_lat, dskip, prm["ssd_norm"].reshape(1, SSD_WIDTH)))

    hd = LRU_HEAD_DIM
    wg = (jnp.concatenate([prm["lru_w_a"][0], prm["lru_w_x"][0],
                           prm["lru_w_a"][1], prm["lru_w_x"][1]], axis=-1) * 0.5).astype(BF16)
    bg = jnp.stack([prm["lru_b_a"][0].reshape(LRU_HEADS, hd), prm["lru_b_x"][0].reshape(LRU_HEADS, hd),
                    prm["lru_b_a"][1].reshape(LRU_HEADS, hd), prm["lru_b_x"][1].reshape(LRU_HEADS, hd)],
                   axis=1).reshape(LRU_HEADS, 1, 4 * hd) * 0.5
    lcw = prm["lru_conv_w"]
    lcb = prm["lru_conv_b"].reshape(1, -1)
    lam = prm["lru_lambda"]
    cw_ = 8
    cr_ = clen // cw_
    xr_ctx = p_ctx[:, :, CTX_COL_XR:CTX_COL_XR + LRU_WIDTH]
    xr_ctx = xr_ctx.reshape(bsz, cw_, cr_, LRU_WIDTH).transpose(0, 2, 1, 3).reshape(bsz, clen, LRU_WIDTH)
    zero_h = jnp.zeros((bsz, 2, LRU_WIDTH), F32)
    (f_ctx,) = _lru(xr_ctx, 0, None, 0, lcw, lcb, wg, bg, lam, zero_h, rows=cr_, width=cw_, emit=False,
                    hps=LRU_CTX_HEADS_PER_STEP)
    rg, _ = _lru(p_lat, COL_XR, p_lat, COL_YR, lcw, lcb, wg, bg, lam, f_ctx,
                 rows=rows, width=GRID_W, emit=True)

    m = _merge_a(yn, rg, prm["w_out_ssd"].astype(BF16), prm["w_out_lru"].astype(BF16), p_lat, *TILES["merge_a"])
    x1, h_ffn = _merge_b(m, prm["w_o"].astype(BF16), x, g_m, sh_f, sc_f, prm["norm_ffn"].reshape(1, d),
                         *TILES["merge_b"])

    act = _ffn_up(h_ffn, prm["ffn_w13"].astype(BF16), *TILES["ffn_up"])
    return _ffn_down(act, prm["ffn_w2"].astype(BF16), x1, g_f, prm["final_norm"].reshape(1, d), *TILES["ffn_down"])


def kernel(x, c, ctx, c_ctx, w_ada, b_ada, norm_mix, norm_ffn, w_in, ssd_conv_w, ssd_conv_b, ssd_dt_bias,
           ssd_a_log, ssd_d, ssd_norm, w_out_ssd, lru_conv_w, lru_conv_b, lru_w_a, lru_b_a, lru_w_x, lru_b_x,
           lru_lambda, w_out_lru, w_gate, b_gate, w_o, ffn_w13, ffn_w2, final_norm):
    assert w_ada.shape[0] == 1, "single-layer trunk"
    bsz = x.shape[0]
    cvec = jnp.zeros((8, D_MODEL), F32).at[:bsz].set(c).at[bsz].set(c_ctx)
    mod_all = _ada(cvec, w_ada[0], b_ada[0].reshape(1, -1))
    prm = dict(w_in=w_in[0], ssd_conv_w=ssd_conv_w[0], ssd_conv_b=ssd_conv_b[0], ssd_dt_bias=ssd_dt_bias[0],
               ssd_a_log=ssd_a_log[0], ssd_d=ssd_d[0], ssd_norm=ssd_norm[0], w_out_ssd=w_out_ssd[0],
               lru_conv_w=lru_conv_w[0], lru_conv_b=lru_conv_b[0], lru_w_a=lru_w_a[0], lru_b_a=lru_b_a[0],
               lru_w_x=lru_w_x[0], lru_b_x=lru_b_x[0], lru_lambda=lru_lambda[0], w_out_lru=w_out_lru[0],
               w_gate=w_gate[0], b_gate=b_gate[0], w_o=w_o[0], norm_mix=norm_mix[0], norm_ffn=norm_ffn[0],
               ffn_w13=ffn_w13[0], ffn_w2=ffn_w2[0], final_norm=final_norm)
    return _layer(x, ctx, mod_all[:bsz], mod_all[bsz], prm)
```

```python
import functools

import jax
import numpy as np
import jax.numpy as jnp
from jax import lax
from jax.experimental import pallas as pl
from jax.experimental.pallas import tpu as pltpu

F32 = jnp.float32
BF16 = jnp.bfloat16

D_MODEL = 2048
GRID_W = 64
N_MOD = 6
SSD_HEAD_DIM = 64
SSD_HEADS = 32
SSD_WIDTH = SSD_HEADS * SSD_HEAD_DIM
SSD_GROUPS = 4
SSD_HPG = SSD_HEADS // SSD_GROUPS
SSD_STATE = 128
SSD_BC = SSD_GROUPS * SSD_STATE
SSD_CHUNK = 128
CONV_W = 4
CONV_PAD_LEFT = 2
LRU_WIDTH = D_MODEL
LRU_HEADS = 16
LRU_HEAD_DIM = LRU_WIDTH // LRU_HEADS
LRU_C = 8.0
EPS = 1e-6

LANES = 128
MXU_COLS = 256
LRU_CTX_HEADS_PER_STEP = 8
HALO = 16
VMEM_LIMIT = 56 * 1024 * 1024

COL_XS = 0
COL_XR = 2048
COL_Z = 4096
COL_YR = 6144
COL_GS = 8192
COL_GR = 10240
COL_BC = 12288
NP = 13312
CTX_COL_BC = 4096
CTX_NP = 5120
DT_COLS = 256


TILES = {"in_proj": (1024,), "merge_a": (512,), "merge_b": (512,), "ffn_up": (2048, 512), "ffn_down": (512,)}


def _params(sem, vmem=VMEM_LIMIT):
    return pltpu.CompilerParams(dimension_semantics=sem, vmem_limit_bytes=vmem)


LOG2E = 1.4426950408889634


def _sigmoid(x):
    return 1.0 / (1.0 + jnp.exp2(x * (-LOG2E)))


def _softplus(x):
    return jnp.maximum(x, 0.0) + jnp.log1p(jnp.exp(-jnp.abs(x)))


def _gelu_tanh(x):
    c1 = 0.7978845608028654
    half_x = 0.5 * x
    return half_x * jnp.tanh(x * (c1 + (c1 * 0.044715) * (x * x))) + half_x


def _norm_mod(x, g, shift, scale):
    xf = x.astype(F32)
    ms = jnp.mean(xf * xf, axis=-1, keepdims=True)
    return (xf * lax.rsqrt(ms + EPS)) * (g * (1.0 + scale)) + shift


def _ada_kernel(c_ref, w_ref, b_ref, o_ref):
    c = c_ref[...]
    s = (c * _sigmoid(c)).astype(BF16)
    o_ref[...] = jnp.dot(s, w_ref[...].astype(BF16), preferred_element_type=F32) + b_ref[...]


def _ada(cvec, w, b):
    m, d = cvec.shape
    n = w.shape[1]
    tn = 1024
    return pl.pallas_call(
        _ada_kernel,
        out_shape=jax.ShapeDtypeStruct((m, n), F32),
        grid=(n // tn,),
        in_specs=[pl.BlockSpec((m, d), lambda j: (0, 0)),
                  pl.BlockSpec((d, tn), lambda j: (0, j)),
                  pl.BlockSpec((1, tn), lambda j: (0, j))],
        out_specs=pl.BlockSpec((m, tn), lambda j: (0, j)),
        compiler_params=_params(("arbitrary",)),
        name="ada",
    )(cvec, w, b)


def _in_proj_kernel(x_ref, sh_ref, sc_ref, g_ref, w_ref, b_ref, wdt_ref, p_ref, dt_ref, hn_ref,
                    *, gate_lo, gate_hi):
    n = pl.program_id(2)

    @pl.when(n == 0)
    def _():
        h = _norm_mod(x_ref[...], g_ref[...], sh_ref[...], sc_ref[...]).astype(BF16)
        hn_ref[...] = h
        dt_ref[...] = jnp.dot(h, wdt_ref[...], preferred_element_type=F32)

    def project(gate):
        hn = hn_ref[...]
        for j in range(0, p_ref.shape[1], MXU_COLS):
            acc = jnp.dot(hn, w_ref[:, j:j + MXU_COLS], preferred_element_type=F32) + b_ref[:, j:j + MXU_COLS]
            p_ref[:, j:j + MXU_COLS] = (_sigmoid(acc) if gate else acc).astype(BF16)

    if gate_lo == gate_hi:
        project(False)
        return
    is_gate = jnp.logical_and(n >= gate_lo, n < gate_hi)
    pl.when(is_gate)(functools.partial(project, True))
    pl.when(jnp.logical_not(is_gate))(functools.partial(project, False))


def _in_proj(x, shift, scale, g, w_cat, b_cat, w_dt, tm, ctx_subset):
    bsz, s, d = x.shape
    tn = 1024
    if ctx_subset:
        n_out = CTX_NP
        gates = (0, 0)
        wtile = lambda n: jnp.where(n < CTX_COL_BC // tn, n, COL_BC // tn)
    else:
        n_out = NP
        gates = (COL_GS // tn, COL_BC // tn)
        wtile = lambda n: n
    kern = functools.partial(_in_proj_kernel, gate_lo=gates[0], gate_hi=gates[1])
    return pl.pallas_call(
        kern,
        out_shape=(jax.ShapeDtypeStruct((bsz, s, n_out), BF16),
                   jax.ShapeDtypeStruct((bsz, s, DT_COLS), F32)),
        grid=(bsz, s // tm, n_out // tn),
        in_specs=[pl.BlockSpec((None, tm, d), lambda b, i, n: (b, i, 0)),
                  pl.BlockSpec((None, 1, d), lambda b, i, n: (b, 0, 0)),
                  pl.BlockSpec((None, 1, d), lambda b, i, n: (b, 0, 0)),
                  pl.BlockSpec((1, d), lambda b, i, n: (0, 0)),
                  pl.BlockSpec((d, tn), lambda b, i, n: (0, wtile(n))),
                  pl.BlockSpec((1, tn), lambda b, i, n: (0, wtile(n))),
                  pl.BlockSpec((d, DT_COLS), lambda b, i, n: (0, 0))],
        out_specs=(pl.BlockSpec((None, tm, tn), lambda b, i, n: (b, i, n)),
                   pl.BlockSpec((None, tm, DT_COLS), lambda b, i, n: (b, i, 0))),
        scratch_shapes=[pltpu.VMEM((tm, d), BF16)],
        compiler_params=_params(("arbitrary", "arbitrary", "arbitrary")),
        name="in_proj",
    )(x, shift, scale, g, w_cat, b_cat, w_dt)


def _split3(v):
    v1 = v.astype(BF16)
    r1 = v - v1.astype(F32)
    v2 = r1.astype(BF16)
    r2 = r1 - v2.astype(F32)
    return v1, v2, r2.astype(BF16)


def _ssd_kernel(*refs, rev, final, nc):
    it = iter(refs)
    if rev:
        xsc_ref, bcc_ref = next(it), next(it)
    else:
        xs_ref, xsp_ref, xsn_ref, bc_ref, bcp_ref, bcn_ref = [next(it) for _ in range(6)]
        cwx_ref, cbx_ref, cwbc_ref, cbbc_ref, shc_ref, sht_ref, shb_ref = [next(it) for _ in range(7)]
    dt_ref, dtb_ref, alog_ref, tri_ref, s0_ref = [next(it) for _ in range(5)]
    if final:
        yf_ref, z_ref, dskip_ref, nw_ref = [next(it) for _ in range(4)]
    y_ref, sfin_ref = next(it), next(it)
    if not rev:
        xsc_out_ref, bcc_out_ref = next(it), next(it)
    st_ref = next(it)
    if final:
        gat_ref = next(it)

    i = pl.program_id(1)
    c = (nc - 1 - i) if rev else i
    q = SSD_CHUNK

    @pl.when(i == 0)
    def _():
        st_ref[...] = s0_ref[...]

    tri = tri_ref[...]

    if rev:
        xs = xsc_ref[...].astype(F32)
        bcv = bcc_ref[...].astype(F32)
    else:
        has_prev = c > 0
        has_next = c < nc - 1
        shift_cur = shc_ref[...]
        shift_top = sht_ref[...]
        shift_bot = shb_ref[...]

        def conv_silu(cur_ref, prev_ref, next_ref, w_ref, b_ref):
            wb = [w_ref[k:k + 1, :].astype(BF16) for k in range(CONV_W)]
            cur = cur_ref[...]
            taps = jnp.concatenate([cur * wb[k] for k in range(CONV_W)], axis=0)
            out = jnp.dot(shift_cur, taps, preferred_element_type=F32) + b_ref[...]
            prev = jnp.where(has_prev, prev_ref[...], jnp.zeros_like(prev_ref))
            top = jnp.concatenate([prev * wb[0], prev * wb[1]], axis=0)
            nxt_rows = jnp.where(has_next, next_ref[...], jnp.zeros_like(next_ref))
            bot = nxt_rows * wb[CONV_W - 1]
            head = out[0:HALO] + jnp.dot(shift_top, top, preferred_element_type=F32)
            tail = out[q - HALO:q] + jnp.dot(shift_bot, bot, preferred_element_type=F32)
            out = jnp.concatenate([head, out[HALO:q - HALO], tail], axis=0)
            return out * _sigmoid(out)

        xs = conv_silu(xs_ref, xsp_ref, xsn_ref, cwx_ref, cbx_ref)
        bcv = conv_silu(bc_ref, bcp_ref, bcn_ref, cwbc_ref, cbbc_ref)
        xsc_out_ref[...] = xs.astype(BF16)
        bcc_out_ref[...] = bcv.astype(BF16)

    dt = _softplus(dt_ref[...] + dtb_ref[...])
    a = dt * (-jnp.exp(alog_ref[...]))

    ii = lax.broadcasted_iota(jnp.int32, (q, q), 0)
    jj = lax.broadcasted_iota(jnp.int32, (q, q), 1)
    mask = (jj >= ii) if rev else (jj <= ii)

    acum = sum(jnp.dot(tri, p, preferred_element_type=F32) for p in _split3(a))
    a_t = a.T
    dt_t = dt.T
    nt = (((1,), (1,)), ((), ()))
    acum_t = sum(lax.dot_general(p, tri, nt, preferred_element_type=F32) for p in _split3(a_t))

    edge = 0 if rev else q - 1
    total = acum[edge:edge + 1, :]
    cd = jnp.exp(total)
    w_t = dt_t * jnp.exp(acum_t[:, edge:edge + 1] - acum_t)
    acum2 = acum * LOG2E
    adj2_t = acum_t * LOG2E - jnp.log2(dt_t)

    lane = lax.broadcasted_iota(jnp.int32, (q, LANES), 1)
    lo_half = lane < SSD_HEAD_DIM
    lo_row = lo_half[0:1, :]

    gsq = jnp.zeros((q, LANES), F32)
    for g in range(SSD_GROUPS):
        b_g = bcv[:, g * SSD_STATE:(g + 1) * SSD_STATE]
        c_g = bcv[:, SSD_BC + g * SSD_STATE:SSD_BC + (g + 1) * SSD_STATE]
        b_gb = b_g.astype(BF16)
        c_gb = c_g.astype(BF16)
        cb = lax.dot_general(c_gb, b_gb, nt, preferred_element_type=F32)
        gw = SSD_HPG * SSD_HEAD_DIM
        s_in = st_ref[:, g * gw:(g + 1) * gw].astype(BF16)
        yoff = jnp.dot(c_gb, s_in, preferred_element_type=F32)
        b_t = b_g.T
        for kp in range(SSD_HPG // 2):
            pair = g * (SSD_HPG // 2) + kp
            col = pair * LANES
            ms, ecols, bws, cds = [], [], [], []
            for e in (2 * pair, 2 * pair + 1):
                colb = jnp.broadcast_to(acum2[:, e:e + 1], (q, q))
                seg = colb - adj2_t[e:e + 1, :]
                lmat_dt = jnp.exp2(jnp.where(mask, seg, -1e30))
                ms.append((cb * lmat_dt).astype(BF16))
                ecols.append(jnp.exp2(colb))
                bws.append((b_t * w_t[e:e + 1, :]).astype(BF16))
                cds.append(jnp.broadcast_to(cd[:, e:e + 1], (1, LANES)))
            xp = xs[:, col:col + LANES]
            rhs = jnp.concatenate([jnp.where(lo_half, xp, 0.0).astype(BF16),
                                   jnp.where(lo_half, 0.0, xp).astype(BF16)], axis=0)
            lhs = jnp.concatenate(ms, axis=1)
            ydiag = jnp.dot(lhs, rhs, preferred_element_type=F32)
            y_pair = ydiag + jnp.where(lo_half, ecols[0], ecols[1]) * yoff[:, kp * LANES:(kp + 1) * LANES]
            cd_pair = jnp.where(lo_row, cds[0], cds[1])
            ds = jnp.dot(jnp.concatenate(bws, axis=1), rhs, preferred_element_type=F32)
            st_ref[:, col:col + LANES] = st_ref[:, col:col + LANES] * cd_pair + ds
            if final:
                y_tot = y_pair + yf_ref[:, col:col + LANES] + dskip_ref[:, col:col + LANES] * xp
                zz = z_ref[:, col:col + LANES].astype(F32)
                gated = y_tot * (zz * _sigmoid(zz))
                gat_ref[:, col:col + LANES] = gated
                gsq = gsq + gated * gated
            else:
                y_ref[:, col:col + LANES] = y_pair

    if final:
        ms = jnp.sum(gsq, axis=-1, keepdims=True) * (1.0 / SSD_WIDTH)
        y_ref[...] = (gat_ref[...] * lax.rsqrt(ms + EPS) * nw_ref[...]).astype(BF16)

    @pl.when(i == nc - 1)
    def _():
        sfin_ref[...] = st_ref[...]


def _ssd_constants(rev):
    q = SSD_CHUNK
    t = np.arange(q)[:, None]
    r = np.arange(CONV_W * q)[None, :]
    shift_cur = (r % q) == t + r // q - CONV_PAD_LEFT
    shift_top = np.zeros((HALO, 2 * HALO), bool)
    shift_top[0, HALO - 2] = shift_top[1, HALO - 1] = shift_top[0, 2 * HALO - 1] = True
    shift_bot = np.zeros((HALO, HALO), bool)
    shift_bot[HALO - 1, 0] = True
    i = np.arange(q)[:, None]
    j = np.arange(q)[None, :]
    tri = (j >= i) if rev else (j <= i)
    return [jnp.asarray(m, BF16) for m in (shift_cur, shift_top, shift_bot, tri)]


def _ssd_sweep(src, dtraw, s0, dtb, alog, *, rev, col_bc=None, conv=None, extra=None):
    bsz, s, _ = dtraw.shape
    nc = s // SSD_CHUNK
    q = SSD_CHUNK
    hb = q // HALO
    final = extra is not None
    assert rev or not final

    def cidx(i):
        return (nc - 1 - i) if rev else i

    def cur(col_block):
        return lambda b, i: (b, cidx(i), col_block)

    def prev(col_block):
        return lambda b, i: (b, jnp.maximum(cidx(i) * hb - 1, 0), col_block)

    def nxt(col_block):
        return lambda b, i: (b, jnp.minimum(cidx(i) * hb + hb, s // HALO - 1), col_block)

    const2 = lambda b, i: (0, 0)
    xw_, bw_ = SSD_WIDTH, 2 * SSD_BC
    shift_cur, shift_top, shift_bot, tri = _ssd_constants(rev)
    if rev:
        in_specs = [pl.BlockSpec((None, q, xw_), cur(0)), pl.BlockSpec((None, q, bw_), cur(0))]
        args = list(src)
    else:
        p = src
        in_specs = [pl.BlockSpec((None, q, xw_), cur(COL_XS // xw_)),
                    pl.BlockSpec((None, HALO, xw_), prev(COL_XS // xw_)),
                    pl.BlockSpec((None, HALO, xw_), nxt(COL_XS // xw_)),
                    pl.BlockSpec((None, q, bw_), cur(col_bc // bw_)),
                    pl.BlockSpec((None, HALO, bw_), prev(col_bc // bw_)),
                    pl.BlockSpec((None, HALO, bw_), nxt(col_bc // bw_)),
                    pl.BlockSpec((CONV_W, xw_), const2),
                    pl.BlockSpec((1, xw_), const2),
                    pl.BlockSpec((CONV_W, bw_), const2),
                    pl.BlockSpec((1, bw_), const2),
                    pl.BlockSpec((q, CONV_W * q), const2),
                    pl.BlockSpec((HALO, 2 * HALO), const2),
                    pl.BlockSpec((HALO, HALO), const2)]
        args = [p, p, p, p, p, p, *conv, shift_cur, shift_top, shift_bot]
    in_specs += [pl.BlockSpec((None, q, LANES), cur(1 if rev else 0)),
                 pl.BlockSpec((1, LANES), const2),
                 pl.BlockSpec((1, LANES), const2),
                 pl.BlockSpec((q, q), const2),
                 pl.BlockSpec((None, SSD_STATE, xw_), lambda b, i: (b, 0, 0))]
    args += [dtraw, dtb, alog, tri, s0]
    scratch = [pltpu.VMEM((SSD_STATE, xw_), F32)]
    if final:
        yf, z_src, dskip, nw = extra
        in_specs += [pl.BlockSpec((None, q, xw_), cur(0)),
                     pl.BlockSpec((None, q, xw_), cur(COL_Z // xw_)),
                     pl.BlockSpec((1, xw_), const2),
                     pl.BlockSpec((1, xw_), const2)]
        args += [yf, z_src, dskip, nw]
        scratch.append(pltpu.VMEM((q, xw_), F32))
    out_shape = [jax.ShapeDtypeStruct((bsz, s, xw_), BF16 if final else F32),
                 jax.ShapeDtypeStruct((bsz, SSD_STATE, xw_), F32)]
    out_specs = [pl.BlockSpec((None, q, xw_), cur(0)),
                 pl.BlockSpec((None, SSD_STATE, xw_), lambda b, i: (b, 0, 0))]
    if not rev:
        out_shape += [jax.ShapeDtypeStruct((bsz, s, xw_), BF16), jax.ShapeDtypeStruct((bsz, s, bw_), BF16)]
        out_specs += [pl.BlockSpec((None, q, xw_), cur(0)), pl.BlockSpec((None, q, bw_), cur(0))]
    kern = functools.partial(_ssd_kernel, rev=rev, final=final, nc=nc)
    return pl.pallas_call(
        kern,
        out_shape=tuple(out_shape),
        grid=(bsz, nc),
        in_specs=in_specs,
        out_specs=tuple(out_specs),
        scratch_shapes=scratch,
        compiler_params=_params(("arbitrary", "arbitrary")),
        name="ssd_" + ("bwd" if rev else "fwd") + ("_final" if final else ""),
    )(*args)


def _lru_kernel(*refs, rows, width, tb, emit, hps):
    n_in = 8 if emit else 7
    n_out = 2 if emit else 1
    per_head_leading = (4, 5) if emit else (3, 4)
    for hh in range(hps):
        lanes = slice(hh * LRU_HEAD_DIM, (hh + 1) * LRU_HEAD_DIM)
        views = [r.at[hh] if k in per_head_leading else r.at[:, lanes] for k, r in enumerate(refs[:n_in + n_out])]
        _lru_head(*views, *refs[n_in + n_out:], rows=rows, width=width, tb=tb, emit=emit)


def _lru_head(*refs, rows, width, tb, emit):
    if emit:
        (xr_ref, yr_ref, cw_ref, cb_ref, wg_ref, bg_ref, lam_ref, h0_ref,
         rg_ref, fin_ref,
         xe_ref, af_ref, uf_ref, ab_ref, ub_ref, sh_ref, fl_ref, pe_ref, cinf_ref, cinb_ref) = refs
    else:
        (xr_ref, cw_ref, cb_ref, wg_ref, bg_ref, lam_ref, h0_ref,
         fin_ref,
         xe_ref, af_ref, uf_ref, ab_ref, ub_ref, sh_ref, fl_ref, pe_ref, cinf_ref, cinb_ref) = refs
    R, W = rows, width
    hd = LRU_HEAD_DIM
    rb = tb // W

    xe_ref[2:2 + R] = xr_ref[...].astype(F32).reshape(R, W, hd)
    zero8 = jnp.zeros((8, hd), F32)
    sh_ref[0:8, :] = zero8
    sh_ref[8 + W:16 + W, :] = zero8
    for src, dst, off in ((R - 2, 0, 7), (R - 1, 1, 7), (0, R + 2, 9)):
        sh_ref[8:8 + W, :] = xe_ref[2 + src]
        xe_ref[dst] = sh_ref[off:off + W, :]

    a_scale = _softplus(-lam_ref[...]) * (-0.5 * LRU_C * LOG2E)
    cwv = cw_ref[...]
    cbv = cb_ref[...]
    wg = wg_ref[...]
    bg = bg_ref[...]

    def gate_body(blk, carry):
        r0 = blk * rb
        xc = cbv
        for k in range(CONV_W):
            xc = xc + cwv[k:k + 1, :] * xe_ref[pl.ds(r0 + k, rb)]
        xc2 = xc.reshape(rb * W, hd)
        pre = jnp.dot(xc2.astype(BF16), wg, preferred_element_type=F32) + bg
        half_x = 0.5 * xc2
        for d, (a_ref, u_ref) in enumerate(((af_ref, uf_ref), (ab_ref, ub_ref))):
            t_r = jnp.tanh(pre[:, (2 * d) * hd:(2 * d + 1) * hd])
            t_i = jnp.tanh(pre[:, (2 * d + 1) * hd:(2 * d + 2) * hd])
            k = a_scale[d:d + 1, :]
            a = jnp.exp2(t_r * k + k)
            v = 1.0 - a * a
            u = (v * lax.rsqrt(jnp.maximum(v, 1e-37))) * (t_i * half_x + half_x)
            a_ref[pl.ds(r0, rb)] = a.reshape(rb, W, hd)
            u_ref[pl.ds(r0, rb)] = u.reshape(rb, W, hd)
        return carry

    lax.fori_loop(0, R // rb, gate_body, 0)

    zeros = jnp.zeros((W, hd), F32)
    ones = jnp.ones((W, hd), F32)

    def scan_step(a_ref, u_ref, r, h, p):
        a = a_ref[r]
        h = a * h + u_ref[r]
        p = a * p
        u_ref[r] = h
        a_ref[r] = p
        return h, p

    def scan_body(t, carry):
        hf, pf, hb, pb = carry
        hf, pf = scan_step(af_ref, uf_ref, t, hf, pf)
        hb, pb = scan_step(ab_ref, ub_ref, R - 1 - t, hb, pb)
        return hf, pf, hb, pb

    def carry_chain(h_end, p_end, h0, cin_ref, reverse):
        fl_ref[...] = h_end
        pe_ref[...] = p_end

        def body(t, carry):
            cc = (W - 1 - t) if reverse else t
            cin_ref[pl.ds(cc, 1), :] = carry
            return fl_ref[pl.ds(cc, 1), :] + pe_ref[pl.ds(cc, 1), :] * carry
        return lax.fori_loop(0, W, body, h0)

    hf_end, pf_end, hb_end, pb_end = lax.fori_loop(0, R, scan_body, (zeros, ones, zeros, ones), unroll=4)
    fin_f = carry_chain(hf_end, pf_end, h0_ref[0:1, :], cinf_ref, False)
    fin_b = carry_chain(hb_end, pb_end, h0_ref[1:2, :], cinb_ref, True)
    fin_ref[0:1, :] = fin_f
    fin_ref[1:2, :] = fin_b

    if emit:
        cin_f = cinf_ref[...]
        cin_b = cinb_ref[...]

        def out_body(r, carry):
            h = (uf_ref[r] + af_ref[r] * cin_f) + (ub_ref[r] + ab_ref[r] * cin_b)
            row0 = pl.multiple_of(r * W, W)
            y = yr_ref[pl.ds(row0, W), :].astype(F32)
            rg_ref[pl.ds(row0, W), :] = (h * _gelu_tanh(y)).astype(BF16)
            return carry

        lax.fori_loop(0, R, out_body, 0, unroll=2)


def _lru(xr_src, xr_col, yr_src, yr_col, cw, cb, wg, bg, lam, h0, *, rows, width, emit, hps=1):
    bsz, s, _ = xr_src.shape
    hd = hps * LRU_HEAD_DIM
    tb = min(1024, s)
    kern = functools.partial(_lru_kernel, rows=rows, width=width, tb=tb, emit=emit, hps=hps)
    head = lambda b, h: (0, h)
    in_specs = [pl.BlockSpec((None, s, hd), lambda b, h: (b, 0, xr_col // hd + h))]
    args = [xr_src]
    if emit:
        in_specs.append(pl.BlockSpec((None, s, hd), lambda b, h: (b, 0, yr_col // hd + h)))
        args.append(yr_src)
    in_specs += [pl.BlockSpec((CONV_W, hd), head),
                 pl.BlockSpec((1, hd), head),
                 pl.BlockSpec((hps, LRU_HEAD_DIM, 4 * LRU_HEAD_DIM), lambda b, h: (h, 0, 0)),
                 pl.BlockSpec((hps, 1, 4 * LRU_HEAD_DIM), lambda b, h: (h, 0, 0)),
                 pl.BlockSpec((2, hd), head),
                 pl.BlockSpec((None, 2, hd), lambda b, h: (b, 0, h))]
    args += [cw, cb, wg, bg, lam, h0]
    out_shape = [jax.ShapeDtypeStruct((bsz, 2, LRU_WIDTH), F32)]
    out_specs = [pl.BlockSpec((None, 2, hd), lambda b, h: (b, 0, h))]
    if emit:
        out_shape.insert(0, jax.ShapeDtypeStruct((bsz, s, LRU_WIDTH), BF16))
        out_specs.insert(0, pl.BlockSpec((None, s, hd), lambda b, h: (b, 0, h)))
    hl = LRU_HEAD_DIM
    big = pltpu.VMEM((rows, width, hl), F32)
    scratch = [pltpu.VMEM((rows + 3, width, hl), F32), big, big, big, big,
               pltpu.VMEM((width + 16, hl), F32),
               pltpu.VMEM((width, hl), F32), pltpu.VMEM((width, hl), F32),
               pltpu.VMEM((width, hl), F32), pltpu.VMEM((width, hl), F32)]
    return pl.pallas_call(
        kern,
        out_shape=tuple(out_shape),
        grid=(bsz, LRU_HEADS // hps),
        in_specs=in_specs,
        out_specs=tuple(out_specs),
        scratch_shapes=scratch,
        compiler_params=_params(("arbitrary", "arbitrary")),
        name="lru" + ("" if emit else "_ctx"),
    )(*args)


def _merge_a_kernel(yn_ref, rg_ref, ws_ref, wl_ref, gs_ref, gr_ref, m_ref):
    yn = yn_ref[...]
    rg = rg_ref[...]
    for j in range(0, m_ref.shape[1], MXU_COLS):
        cols = slice(j, j + MXU_COLS)
        o_s = jnp.dot(yn, ws_ref[:, cols], preferred_element_type=F32)
        o_r = jnp.dot(rg, wl_ref[:, cols], preferred_element_type=F32)
        m_ref[:, cols] = (gs_ref[:, cols].astype(F32) * o_s + gr_ref[:, cols].astype(F32) * o_r).astype(BF16)


def _merge_a(yn, rg, ws, wl, p, tm):
    bsz, s, d = yn.shape
    row = lambda b, i: (b, i, 0)
    whole = lambda b, i: (0, 0)
    return pl.pallas_call(
        _merge_a_kernel,
        out_shape=jax.ShapeDtypeStruct((bsz, s, d), BF16),
        grid=(bsz, s // tm),
        in_specs=[pl.BlockSpec((None, tm, d), row),
                  pl.BlockSpec((None, tm, d), row),
                  pl.BlockSpec((d, d), whole),
                  pl.BlockSpec((d, d), whole),
                  pl.BlockSpec((None, tm, d), lambda b, i: (b, i, COL_GS // d)),
                  pl.BlockSpec((None, tm, d), lambda b, i: (b, i, COL_GR // d))],
        out_specs=pl.BlockSpec((None, tm, d), row),
        compiler_params=_params(("arbitrary", "arbitrary")),
        name="merge_a",
    )(yn, rg, ws, wl, p, p)


def _merge_b_kernel(m_ref, w_ref, x_ref, gm_ref, sh_ref, sc_ref, g_ref, x1_ref, h_ref):
    half = m_ref.shape[0] // 2
    for r in (0, half):
        rows = slice(r, r + half)
        mix = jnp.dot(m_ref[rows, :], w_ref[...], preferred_element_type=F32)
        x1 = x_ref[rows, :] + gm_ref[...] * mix
        x1_ref[rows, :] = x1
        h_ref[rows, :] = _norm_mod(x1, g_ref[...], sh_ref[...], sc_ref[...]).astype(BF16)


def _merge_b(m, w_o, x, gm, shift, scale, g, tm):
    bsz, s, d = m.shape
    row = lambda b, i: (b, i, 0)
    vec = lambda b, i: (b, 0, 0)
    return pl.pallas_call(
        _merge_b_kernel,
        out_shape=(jax.ShapeDtypeStruct((bsz, s, d), F32), jax.ShapeDtypeStruct((bsz, s, d), BF16)),
        grid=(bsz, s // tm),
        in_specs=[pl.BlockSpec((None, tm, d), row),
                  pl.BlockSpec((d, d), lambda b, i: (0, 0)),
                  pl.BlockSpec((None, tm, d), row),
                  pl.BlockSpec((None, 1, d), vec),
                  pl.BlockSpec((None, 1, d), vec),
                  pl.BlockSpec((None, 1, d), vec),
                  pl.BlockSpec((1, d), lambda b, i: (0, 0))],
        out_specs=(pl.BlockSpec((None, tm, d), row), pl.BlockSpec((None, tm, d), row)),
        compiler_params=_params(("arbitrary", "arbitrary")),
        name="merge_b",
    )(m, w_o, x, gm, shift, scale, g)


def _ffn_up_kernel(h_ref, w1_ref, w3_ref, a_ref):
    h = h_ref[...]
    for j in range(0, a_ref.shape[1], MXU_COLS):
        cols = slice(j, j + MXU_COLS)
        gate = jnp.dot(h, w1_ref[:, cols], preferred_element_type=F32)
        up = jnp.dot(h, w3_ref[:, cols], preferred_element_type=F32)
        a_ref[:, cols] = (gate * _sigmoid(gate) * up).astype(BF16)


def _ffn_up(h, w13, tm, tn):
    bsz, s, d = h.shape
    hid = w13.shape[1] // 2
    nb = hid // tn
    return pl.pallas_call(
        _ffn_up_kernel,
        out_shape=jax.ShapeDtypeStruct((bsz, s, hid), BF16),
        grid=(bsz, s // tm, nb),
        in_specs=[pl.BlockSpec((None, tm, d), lambda b, i, n: (b, i, 0)),
                  pl.BlockSpec((d, tn), lambda b, i, n: (0, n)),
                  pl.BlockSpec((d, tn), lambda b, i, n: (0, nb + n))],
        out_specs=pl.BlockSpec((None, tm, tn), lambda b, i, n: (b, i, n)),
        compiler_params=_params(("arbitrary", "arbitrary", "arbitrary")),
        name="ffn_up",
    )(h, w13, w13)


def _ffn_down_kernel(a_ref, w_ref, x_ref, gf_ref, fn_ref, o_ref):
    x2 = x_ref[...] + gf_ref[...] * jnp.dot(a_ref[...], w_ref[...], preferred_element_type=F32)
    ms = jnp.mean(x2 * x2, axis=-1, keepdims=True)
    o_ref[...] = x2 * lax.rsqrt(ms + EPS) * fn_ref[...]


def _ffn_down(a, w2, x1, gf, fnorm, tm):
    bsz, s, hid = a.shape
    d = w2.shape[1]
    row = lambda b, i: (b, i, 0)
    return pl.pallas_call(
        _ffn_down_kernel,
        out_shape=jax.ShapeDtypeStruct((bsz, s, d), F32),
        grid=(bsz, s // tm),
        in_specs=[pl.BlockSpec((None, tm, hid), row),
                  pl.BlockSpec((hid, d), lambda b, i: (0, 0)),
                  pl.BlockSpec((None, tm, d), row),
                  pl.BlockSpec((None, 1, d), lambda b, i: (b, 0, 0)),
                  pl.BlockSpec((1, d), lambda b, i: (0, 0))],
        out_specs=pl.BlockSpec((None, tm, d), row),
        compiler_params=_params(("arbitrary", "arbitrary")),
        name="ffn_down",
    )(a, w2, x1, gf, fnorm)


def _pad_lanes(v, n=LANES):
    return jnp.pad(v, (0, n - v.shape[0])).reshape(1, n)


def _layer(x, ctx, mod, mod_c, prm):
    bsz, seq, d = x.shape
    clen = ctx.shape[1]
    rows = seq // GRID_W

    sh_m, sc_m, g_m, sh_f, sc_f, g_f = [m.reshape(bsz, 1, d) for m in jnp.split(mod, N_MOD, axis=-1)]
    csh_m, csc_m = [m.reshape(1, 1, d) for m in jnp.split(mod_c, N_MOD, axis=-1)[:2]]

    w_in = prm["w_in"]
    o = 0
    parts = {}
    for name, size in (("z", SSD_WIDTH), ("xs", SSD_WIDTH), ("b", SSD_BC), ("c", SSD_BC),
                       ("dt", 2 * SSD_HEADS), ("xr", LRU_WIDTH), ("yr", LRU_WIDTH)):
        parts[name] = w_in[:, o:o + size]
        o += size
    w_cat = jnp.concatenate([parts[k].astype(BF16) for k in ("xs", "xr", "z", "yr")]
                            + [prm["w_gate"].astype(BF16), parts["b"].astype(BF16), parts["c"].astype(BF16)], axis=1)
    b_cat = jnp.concatenate([jnp.zeros((COL_GS,), F32), prm["b_gate"],
                             jnp.zeros((NP - COL_BC,), F32)]).reshape(1, NP)
    w_dt = jnp.zeros((d, DT_COLS), F32)
    w_dt = w_dt.at[:, 0:SSD_HEADS].set(parts["dt"][:, :SSD_HEADS])
    w_dt = w_dt.at[:, LANES:LANES + SSD_HEADS].set(parts["dt"][:, SSD_HEADS:]).astype(BF16)
    g_mix = prm["norm_mix"].reshape(1, d)

    p_ctx, dt_ctx = _in_proj(ctx.reshape(1, bsz * clen, d), csh_m, csc_m, g_mix, w_cat, b_cat, w_dt,
                             tm=bsz * clen, ctx_subset=True)
    p_ctx = p_ctx.reshape(bsz, clen, CTX_NP)
    dt_ctx = dt_ctx.reshape(bsz, clen, DT_COLS)
    p_lat, dt_lat = _in_proj(x, sh_m, sc_m, g_mix, w_cat, b_cat, w_dt, *TILES["in_proj"], ctx_subset=False)

    cwx = prm["ssd_conv_w"][:, :SSD_WIDTH]
    cwbc = prm["ssd_conv_w"][:, SSD_WIDTH:]
    cbx = prm["ssd_conv_b"][:SSD_WIDTH].reshape(1, -1)
    cbbc = prm["ssd_conv_b"][SSD_WIDTH:].reshape(1, -1)
    dtb = [_pad_lanes(prm["ssd_dt_bias"][k]) for k in range(2)]
    alog = [_pad_lanes(prm["ssd_a_log"][k]) for k in range(2)]
    conv = (cwx, cbx, cwbc, cbbc)
    zero_state = jnp.zeros((bsz, SSD_STATE, SSD_WIDTH), F32)
    _, s_f, xc_ctx, bc_ctx = _ssd_sweep(p_ctx, dt_ctx, zero_state, dtb[0], alog[0], rev=False,
                                        col_bc=CTX_COL_BC, conv=conv)
    _, s_b = _ssd_sweep((xc_ctx, bc_ctx), dt_ctx, zero_state, dtb[1], alog[1], rev=True)
    y_f, _, xc_lat, bc_lat = _ssd_sweep(p_lat, dt_lat, s_f, dtb[0], alog[0], rev=False, col_bc=COL_BC, conv=conv)
    dskip = jnp.repeat(prm["ssd_d"], SSD_HEAD_DIM).reshape(1, SSD_WIDTH)
    yn, _ = _ssd_sweep((xc_lat, bc_lat), dt_lat, s_b, dtb[1], alog[1], rev=True,
                       extra=(y_f, p_lat, dskip, prm["ssd_norm"].reshape(1, SSD_WIDTH)))

    hd = LRU_HEAD_DIM
    wg = (jnp.concatenate([prm["lru_w_a"][0], prm["lru_w_x"][0],
                           prm["lru_w_a"][1], prm["lru_w_x"][1]], axis=-1) * 0.5).astype(BF16)
    bg = jnp.stack([prm["lru_b_a"][0].reshape(LRU_HEADS, hd), prm["lru_b_x"][0].reshape(LRU_HEADS, hd),
                    prm["lru_b_a"][1].reshape(LRU_HEADS, hd), prm["lru_b_x"][1].reshape(LRU_HEADS, hd)],
                   axis=1).reshape(LRU_HEADS, 1, 4 * hd) * 0.5
    lcw = prm["lru_conv_w"]
    lcb = prm["lru_conv_b"].reshape(1, -1)
    lam = prm["lru_lambda"]
    cw_ = 8
    cr_ = clen // cw_
    xr_ctx = p_ctx[:, :, COL_XR:COL_XR + LRU_WIDTH]
    xr_ctx = xr_ctx.reshape(bsz, cw_, cr_, LRU_WIDTH).transpose(0, 2, 1, 3).reshape(bsz, clen, LRU_WIDTH)
    zero_h = jnp.zeros((bsz, 2, LRU_WIDTH), F32)
    (f_ctx,) = _lru(xr_ctx, 0, None, 0, lcw, lcb, wg, bg, lam, zero_h, rows=cr_, width=cw_, emit=False,
                    hps=LRU_CTX_HEADS_PER_STEP)
    rg, _ = _lru(p_lat, COL_XR, p_lat, COL_YR, lcw, lcb, wg, bg, lam, f_ctx,
                 rows=rows, width=GRID_W, emit=True)

    m = _merge_a(yn, rg, prm["w_out_ssd"].astype(BF16), prm["w_out_lru"].astype(BF16), p_lat, *TILES["merge_a"])
    x1, h_ffn = _merge_b(m, prm["w_o"].astype(BF16), x, g_m, sh_f, sc_f, prm["norm_ffn"].reshape(1, d),
                         *TILES["merge_b"])

    act = _ffn_up(h_ffn, prm["ffn_w13"].astype(BF16), *TILES["ffn_up"])
    return _ffn_down(act, prm["ffn_w2"].astype(BF16), x1, g_f, prm["final_norm"].reshape(1, d), *TILES["ffn_down"])


def kernel(x, c, ctx, c_ctx, w_ada, b_ada, norm_mix, norm_ffn, w_in, ssd_conv_w, ssd_conv_b, ssd_dt_bias,
           ssd_a_log, ssd_d, ssd_norm, w_out_ssd, lru_conv_w, lru_conv_b, lru_w_a, lru_b_a, lru_w_x, lru_b_x,
           lru_lambda, w_out_lru, w_gate, b_gate, w_o, ffn_w13, ffn_w2, final_norm):
    assert w_ada.shape[0] == 1, "single-layer trunk"
    bsz = x.shape[0]
    cvec = jnp.zeros((8, D_MODEL), F32).at[:bsz].set(c).at[bsz].set(c_ctx)
    mod_all = _ada(cvec, w_ada[0], b_ada[0].reshape(1, -1))
    prm = dict(w_in=w_in[0], ssd_conv_w=ssd_conv_w[0], ssd_conv_b=ssd_conv_b[0], ssd_dt_bias=ssd_dt_bias[0],
               ssd_a_log=ssd_a_log[0], ssd_d=ssd_d[0], ssd_norm=ssd_norm[0], w_out_ssd=w_out_ssd[0],
               lru_conv_w=lru_conv_w[0], lru_conv_b=lru_conv_b[0], lru_w_a=lru_w_a[0], lru_b_a=lru_b_a[0],
               lru_w_x=lru_w_x[0], lru_b_x=lru_b_x[0], lru_lambda=lru_lambda[0], w_out_lru=w_out_lru[0],
               w_gate=w_gate[0], b_gate=b_gate[0], w_o=w_o[0], norm_mix=norm_mix[0], norm_ffn=norm_ffn[0],
               ffn_w13=ffn_w13[0], ffn_w2=ffn_w2[0], final_norm=final_norm)
    return _layer(x, ctx, mod_all[:bsz], mod_all[bsz], prm)
```
